```python
import jax, jax.numpy as jnp
from jax import lax
import numpy as np

D_MODEL = 2048
BATCH = 2
SEQ = 4096
DEPTH = 2
DEC_BATCH = 16
DEC_SEQ = 16
PAST_LEN = 2048

CHUNK = 64
N_PREV_CHUNKS = 8
BAND = (N_PREV_CHUNKS + 1) * CHUNK
N_HEADS = 16
HEAD_DIM = D_MODEL // N_HEADS
REL_MAX = 4 * CHUNK
Q_BLOCK = 128
N_EXPERTS = 64
TOP_K = 8
N_GROUPS = 8
TOPK_GROUPS = 4
D_EXPERT = D_MODEL // 8
D_SHARED = D_MODEL // 8
ROUTED_SCALE = 2.5
MOE_BLOCK = 128
N_A_LAYERS = (DEPTH + 1) // 2
N_B_LAYERS = DEPTH // 2
NORM_EPS = 1e-6
NEG_INF = -1e30
ATTN_SCALE = HEAD_DIM ** -0.5

kernel_name = 'hybrid_streaming_chunkband_fox_moe_step'


def rms_norm(x, g):
    xf = x.astype(jnp.float32)
    y = xf * lax.rsqrt(jnp.mean(xf * xf, axis=-1, keepdims=True) + NORM_EPS)
    return (y * g.astype(jnp.float32)).astype(x.dtype)


def split_heads(t):
    return t.reshape(t.shape[:-1] + (N_HEADS, HEAD_DIM))


def adaln(c, w, b):
    return jnp.split(jax.nn.silu(c) @ w + b, 6, axis=-1)


def modulate(x, g, shift, scale):
    return rms_norm(x, g) * (1 + scale[:, None, :]) + shift[:, None, :]


def rel_bias(table, rel):
    return table[:, np.clip(rel, -REL_MAX, REL_MAX) + REL_MAX].astype(jnp.float32)


def chunk_qkv(h, w_in, q_g, k_g):
    q, k, v = jnp.split(h @ w_in, 3, axis=-1)
    return rms_norm(split_heads(q), q_g), rms_norm(split_heads(k), k_g), split_heads(v)


def chunk_attn_prompt(h, w_in, q_g, k_g, rel_table, w_out):
    bsz, seq, _ = h.shape
    n_chunks = seq // CHUNK
    q, k, v = chunk_qkv(h, w_in, q_g, k_g)
    pad = ((0, 0), (N_PREV_CHUNKS, 0), (0, 0), (0, 0), (0, 0))
    kp = jnp.pad(k.reshape(bsz, n_chunks, CHUNK, N_HEADS, HEAD_DIM), pad)
    vp = jnp.pad(v.reshape(bsz, n_chunks, CHUNK, N_HEADS, HEAD_DIM), pad)
    k_band = jnp.concatenate([kp[:, j:j + n_chunks] for j in range(N_PREV_CHUNKS + 1)], axis=2)
    v_band = jnp.concatenate([vp[:, j:j + n_chunks] for j in range(N_PREV_CHUNKS + 1)], axis=2)
    qc = q.reshape(bsz, n_chunks, CHUNK, N_HEADS, HEAD_DIM)
    s = jnp.einsum('bcqhd,bckhd->bchqk', qc, k_band).astype(jnp.float32) * ATTN_SCALE
    rel = N_PREV_CHUNKS * CHUNK + np.arange(CHUNK)[:, None] - np.arange(BAND)[None, :]
    s = s + rel_bias(rel_table, rel)[None, None]
    key_chunk = np.arange(n_chunks)[:, None] - N_PREV_CHUNKS + np.arange(BAND)[None, :] // CHUNK
    s = jnp.where((key_chunk >= 0)[None, :, None, None, :], s, NEG_INF)
    p = jax.nn.softmax(s, axis=-1).astype(v.dtype)
    o = jnp.einsum('bchqk,bckhd->bcqhd', p, v_band).reshape(bsz, seq, D_MODEL)
    keep = min(N_PREV_CHUNKS * CHUNK, seq)
    return o @ w_out, k[:, seq - keep:], v[:, seq - keep:]


def chunk_attn_sample(h, cache_k, cache_v, w_in, q_g, k_g, rel_table, w_out):
    bsz, t, _ = h.shape
    q, k, v = chunk_qkv(h, w_in, q_g, k_g)
    w = cache_k.shape[1]
    kk = jnp.concatenate([cache_k.astype(k.dtype), k], axis=1)
    vv = jnp.concatenate([cache_v.astype(v.dtype), v], axis=1)
    s = jnp.einsum('bqhd,bkhd->bhqk', q, kk).astype(jnp.float32) * ATTN_SCALE
    rel = np.arange(t)[:, None] + w - np.arange(w + t)[None, :]
    s = s + rel_bias(rel_table, rel)[None]
    p = jax.nn.softmax(s, axis=-1).astype(vv.dtype)
    o = jnp.einsum('bhqk,bkhd->bqhd', p, vv).reshape(bsz, t, D_MODEL)
    return o @ w_out, k, v


def fox_proj(h, w_in, f_bias, q_g, k_g):
    p = h @ w_in
    d = D_MODEL
    q = rms_norm(split_heads(p[..., :d]), q_g)
    k = rms_norm(split_heads(p[..., d:2 * d]), k_g)
    v = split_heads(p[..., 2 * d:3 * d])
    logf = jax.nn.log_sigmoid(p[..., 3 * d:3 * d + N_HEADS].astype(jnp.float32) + f_bias.astype(jnp.float32))
    g = p[..., 3 * d + N_HEADS:]
    return q, k, v, logf, g


def fox_prompt(h, w_in, f_bias, q_g, k_g, w_out):
    bsz, seq, _ = h.shape
    q, k, v, logf, g = fox_proj(h, w_in, f_bias, q_g, k_g)
    cum = jnp.cumsum(logf, axis=1)
    cum_t = cum.transpose(0, 2, 1)
    n_blk = seq // Q_BLOCK
    q_blocks = q.reshape(bsz, n_blk, Q_BLOCK, N_HEADS, HEAD_DIM).transpose(1, 0, 2, 3, 4)
    c_blocks = cum.reshape(bsz, n_blk, Q_BLOCK, N_HEADS).transpose(1, 0, 3, 2)
    kpos = jnp.arange(seq)

    def block(args):
        qi, ci, n = args
        s = jnp.einsum('bqhd,bkhd->bhqk', qi, k).astype(jnp.float32) * ATTN_SCALE
        s = s + ci[..., None] - cum_t[:, :, None, :]
        qpos = n * Q_BLOCK + jnp.arange(Q_BLOCK)
        s = jnp.where((kpos[None, :] <= qpos[:, None])[None, None], s, NEG_INF)
        pr = jax.nn.softmax(s, axis=-1).astype(v.dtype)
        return jnp.einsum('bhqk,bkhd->bqhd', pr, v)

    o = lax.map(block, (q_blocks, c_blocks, jnp.arange(n_blk)))
    o = o.transpose(1, 0, 2, 3, 4).reshape(bsz, seq, D_MODEL)
    y = (o * jax.nn.sigmoid(g)) @ w_out
    return y, k, v, logf.astype(h.dtype)


def fox_sample(h, cache_k, cache_v, cache_logf, w_in, f_bias, q_g, k_g, w_out):
    bsz, t, _ = h.shape
    q, k, v, logf, g = fox_proj(h, w_in, f_bias, q_g, k_g)
    past = cache_k.shape[1]
    kk = jnp.concatenate([cache_k.astype(k.dtype), k], axis=1)
    vv = jnp.concatenate([cache_v.astype(v.dtype), v], axis=1)
    lf = jnp.concatenate([cache_logf.astype(jnp.float32), logf], axis=1)
    cum = jnp.cumsum(lf, axis=1).transpose(0, 2, 1)
    s = jnp.einsum('bqhd,bkhd->bhqk', q, kk).astype(jnp.float32) * ATTN_SCALE
    s = s + cum[:, :, past:, None] - cum[:, :, None, :]
    causal = np.arange(past + t)[None, :] <= past + np.arange(t)[:, None]
    s = jnp.where(causal[None, None], s, NEG_INF)
    pr = jax.nn.softmax(s, axis=-1).astype(vv.dtype)
    o = jnp.einsum('bhqk,bkhd->bqhd', pr, vv).reshape(bsz, t, D_MODEL)
    y = (o * jax.nn.sigmoid(g)) @ w_out
    return y, k, v, logf.astype(h.dtype)


def moe_tokens(t, router_w, router_bias, wg, wu, wd, sg, su, sd):
    n_tok = t.shape[0]
    scores = jax.nn.sigmoid((t @ router_w).astype(jnp.float32))
    sel = scores + router_bias.astype(jnp.float32)
    grp = lax.top_k(sel.reshape(n_tok, N_GROUPS, N_EXPERTS // N_GROUPS), 2)[0].sum(-1)
    _, gidx = lax.top_k(grp, TOPK_GROUPS)
    gmask = jax.nn.one_hot(gidx, N_GROUPS, dtype=jnp.float32).sum(1)
    emask = jnp.repeat(gmask, N_EXPERTS // N_GROUPS, axis=-1) > 0
    _, eidx = lax.top_k(jnp.where(emask, sel, NEG_INF), TOP_K)
    w = jnp.take_along_axis(scores, eidx, axis=-1)
    w = w / jnp.sum(w, axis=-1, keepdims=True) * ROUTED_SCALE
    gates = jnp.einsum('tk,tke->te', w, jax.nn.one_hot(eidx, N_EXPERTS, dtype=jnp.float32)).astype(t.dtype)
    a = jnp.einsum('td,edf->tef', t, wg)
    b = jnp.einsum('td,edf->tef', t, wu)
    routed = jnp.einsum('tef,efd->td', jax.nn.silu(a) * b * gates[:, :, None], wd)
    shared = (jax.nn.silu(t @ sg) * (t @ su)) @ sd
    return routed + shared


def moe_ffn(h, router_w, router_bias, wg, wu, wd, sg, su, sd):
    t = h.reshape(-1, D_MODEL)
    n_tok = t.shape[0]
    f = lambda blk: moe_tokens(blk, router_w, router_bias, wg, wu, wd, sg, su, sd)
    if n_tok % MOE_BLOCK == 0:
        out = lax.map(f, t.reshape(-1, MOE_BLOCK, D_MODEL)).reshape(n_tok, D_MODEL)
    else:
        out = f(t)
    return out.reshape(h.shape)


def setup_inputs(seed: int = 0) -> dict:
    key = jax.random.key(seed)
    ks = jax.random.split(key, 40)
    nrm = lambda k, shape, s=1.0: s * jax.random.normal(k, shape, jnp.float32)
    d = D_MODEL
    a_rows = min(N_PREV_CHUNKS * CHUNK, PAST_LEN)
    inv = d ** -0.5
    return {
        'x_prompt': nrm(ks[0], (BATCH, SEQ, d)),
        'x_sample': nrm(ks[1], (DEC_BATCH, DEC_SEQ, d)),
        'c_prompt': nrm(ks[2], (BATCH, d)),
        'c_sample': nrm(ks[3], (DEC_BATCH, d)),
        'cache_a_k': nrm(ks[4], (N_A_LAYERS, DEC_BATCH, a_rows, N_HEADS, HEAD_DIM)),
        'cache_a_v': nrm(ks[5], (N_A_LAYERS, DEC_BATCH, a_rows, N_HEADS, HEAD_DIM)),
        'cache_b_k': nrm(ks[6], (N_B_LAYERS, DEC_BATCH, PAST_LEN, N_HEADS, HEAD_DIM)),
        'cache_b_v': nrm(ks[7], (N_B_LAYERS, DEC_BATCH, PAST_LEN, N_HEADS, HEAD_DIM)),
        'cache_b_logf': jax.nn.log_sigmoid(2.0 + nrm(ks[8], (N_B_LAYERS, DEC_BATCH, PAST_LEN, N_HEADS), 0.5)),
        'ada_w': nrm(ks[9], (DEPTH, d, 6 * d), 0.5 * inv),
        'ada_b': nrm(ks[10], (DEPTH, 6 * d), 0.1),
        'norm1_g': 1.0 + nrm(ks[11], (DEPTH, d), 0.05),
        'norm2_g': 1.0 + nrm(ks[12], (DEPTH, d), 0.05),
        'a_w_in': nrm(ks[13], (N_A_LAYERS, d, 3 * d), inv),
        'a_q_g': 1.0 + nrm(ks[14], (N_A_LAYERS, HEAD_DIM), 0.05),
        'a_k_g': 1.0 + nrm(ks[15], (N_A_LAYERS, HEAD_DIM), 0.05),
        'a_rel_table': nrm(ks[16], (N_A_LAYERS, N_HEADS, 2 * REL_MAX + 1), 0.2),
        'a_w_out': nrm(ks[17], (N_A_LAYERS, d, d), inv),
        'b_w_in': nrm(ks[18], (N_B_LAYERS, d, 4 * d + N_HEADS), inv),
        'b_f_bias': 2.0 + nrm(ks[19], (N_B_LAYERS, N_HEADS), 0.5),
        'b_q_g': 1.0 + nrm(ks[20], (N_B_LAYERS, HEAD_DIM), 0.05),
        'b_k_g': 1.0 + nrm(ks[21], (N_B_LAYERS, HEAD_DIM), 0.05),
        'b_w_out': nrm(ks[22], (N_B_LAYERS, d, d), inv),
        'router_w': nrm(ks[23], (DEPTH, d, N_EXPERTS), inv),
        'router_bias': nrm(ks[24], (DEPTH, N_EXPERTS), 0.01),
        'exp_w_gate': nrm(ks[25], (DEPTH, N_EXPERTS, d, D_EXPERT), inv),
        'exp_w_up': nrm(ks[26], (DEPTH, N_EXPERTS, d, D_EXPERT), inv),
        'exp_w_down': nrm(ks[27], (DEPTH, N_EXPERTS, D_EXPERT, d), D_EXPERT ** -0.5),
        'sh_w_gate': nrm(ks[28], (DEPTH, d, D_SHARED), inv),
        'sh_w_up': nrm(ks[29], (DEPTH, d, D_SHARED), inv),
        'sh_w_down': nrm(ks[30], (DEPTH, D_SHARED, d), D_SHARED ** -0.5),
    }


def reference(x_prompt, x_sample, c_prompt, c_sample, cache_a_k, cache_a_v, cache_b_k, cache_b_v,
              cache_b_logf, ada_w, ada_b, norm1_g, norm2_g, a_w_in, a_q_g, a_k_g, a_rel_table, a_w_out,
              b_w_in, b_f_bias, b_q_g, b_k_g, b_w_out, router_w, router_bias, exp_w_gate, exp_w_up,
              exp_w_down, sh_w_gate, sh_w_up, sh_w_down):
    xp, xs = x_prompt, x_sample
    akp, avp, aks, avs = [], [], [], []
    bkp, bvp, blp, bks, bvs, bls = [], [], [], [], [], []
    for i in range(DEPTH):
        mp = adaln(c_prompt, ada_w[i], ada_b[i])
        ms = adaln(c_sample, ada_w[i], ada_b[i])
        hp = modulate(xp, norm1_g[i], mp[0], mp[1])
        hs = modulate(xs, norm1_g[i], ms[0], ms[1])
        j = i // 2
        if i % 2 == 0:
            yp, kp_, vp_ = chunk_attn_prompt(hp, a_w_in[j], a_q_g[j], a_k_g[j], a_rel_table[j], a_w_out[j])
            ys, ks_, vs_ = chunk_attn_sample(hs, cache_a_k[j], cache_a_v[j], a_w_in[j], a_q_g[j], a_k_g[j],
                                             a_rel_table[j], a_w_out[j])
            akp.append(kp_); avp.append(vp_); aks.append(ks_); avs.append(vs_)
        else:
            yp, kp_, vp_, lp_ = fox_prompt(hp, b_w_in[j], b_f_bias[j], b_q_g[j], b_k_g[j], b_w_out[j])
            ys, ks_, vs_, ls_ = fox_sample(hs, cache_b_k[j], cache_b_v[j], cache_b_logf[j], b_w_in[j],
                                           b_f_bias[j], b_q_g[j], b_k_g[j], b_w_out[j])
            bkp.append(kp_); bvp.append(vp_); blp.append(lp_)
            bks.append(ks_); bvs.append(vs_); bls.append(ls_)
        xp = xp + mp[2][:, None, :] * yp
        xs = xs + ms[2][:, None, :] * ys
        hp = modulate(xp, norm2_g[i], mp[3], mp[4])
        hs = modulate(xs, norm2_g[i], ms[3], ms[4])
        moe_args = (router_w[i], router_bias[i], exp_w_gate[i], exp_w_up[i], exp_w_down[i],
                    sh_w_gate[i], sh_w_up[i], sh_w_down[i])
        xp = xp + mp[5][:, None, :] * moe_ffn(hp, *moe_args)
        xs = xs + ms[5][:, None, :] * moe_ffn(hs, *moe_args)
    a_k_prompt = jnp.stack(akp)
    a_v_prompt = jnp.stack(avp)
    a_k_sample = jnp.stack(aks)
    a_v_sample = jnp.stack(avs)
    b_k_prompt = jnp.stack(bkp)
    b_v_prompt = jnp.stack(bvp)
    b_logf_prompt = jnp.stack(blp)
    b_k_sample = jnp.stack(bks)
    b_v_sample = jnp.stack(bvs)
    b_logf_sample = jnp.stack(bls)
    return (xp, xs, a_k_prompt, a_v_prompt, a_k_sample, a_v_sample, b_k_prompt, b_v_prompt,
            b_logf_prompt, b_k_sample, b_v_sample, b_logf_sample)
```

```python
import functools

import jax
import jax.numpy as jnp
import numpy as np
from jax import lax
from jax.experimental import pallas as pl
from jax.experimental.pallas import tpu as pltpu

F32 = jnp.float32
BF16 = jnp.bfloat16

CHUNK = 64
N_PREV_CHUNKS = 8
TOP_K = 8
N_GROUPS = 8
TOPK_GROUPS = 4
ROUTED_SCALE = 2.5
NORM_EPS = 1e-6
NEG_INF = -1e30
REMOVED = -3e38
LANE = 128
VMEM_LIMIT = 56 * 1024 * 1024
Q_TILE = 128
BAND_KEYS = (N_PREV_CHUNKS + 2) * CHUNK
FOX_TILE = 256
MOE_TILE = 256


def _pick(n, cands):
    for c in cands:
        if n % c == 0:
            return c
    return n


def _params(sem):
    return pltpu.CompilerParams(dimension_semantics=sem, vmem_limit_bytes=VMEM_LIMIT)


def _nt(a, b):
    return lax.dot_general(a, b, (((1,), (1,)), ((), ())), preferred_element_type=F32)


def _dot(a, b):
    return jnp.dot(a, b, preferred_element_type=F32)


def _mod_spec(mod, tm, tn, rows_per_seq, row_arg, col_arg):
    if mod.ndim == 3:
        def imap(*g):
            col = 0 if col_arg is None else g[col_arg]
            return ((g[row_arg] * tm) // rows_per_seq, 0, col)
        return pl.BlockSpec((None, 1, tn), imap)

    def imap2(*g):
        col = 0 if col_arg is None else g[col_arg]
        return (g[row_arg], col)
    return pl.BlockSpec((tm, tn), imap2)


def _adaln_kernel(c_ref, w_ref, b_ref, o_ref):
    c = c_ref[...]
    a = (c * jax.nn.sigmoid(c)).astype(BF16)
    o_ref[...] = _dot(a, w_ref[...].astype(BF16)) + b_ref[...]


def _adaln(c_all, ada_w, ada_b3, layer):
    r, d = c_all.shape
    n = ada_w.shape[2]
    tn = _pick(n, (1024, 512, 256, 128))
    return pl.pallas_call(
        _adaln_kernel,
        grid=(n // tn,),
        in_specs=[pl.BlockSpec((r, d), lambda j: (0, 0)),
                  pl.BlockSpec((None, d, tn), lambda j: (layer, 0, j)),
                  pl.BlockSpec((None, 1, tn), lambda j: (layer, 0, j))],
        out_specs=pl.BlockSpec((r, tn), lambda j: (0, j)),
        out_shape=jax.ShapeDtypeStruct((r, n), F32),
        compiler_params=_params(("arbitrary",)),
        name="adaln",
    )(c_all, ada_w, ada_b3)


def _modulated(x, g, sh, sc):
    ms = jnp.mean(x * x, axis=-1, keepdims=True)
    y = x * lax.rsqrt(ms + NORM_EPS) * g
    return y * (1.0 + sc) + sh


def _modulate_kernel(x_ref, g_ref, sh_ref, sc_ref, o_ref):
    o_ref[...] = _modulated(x_ref[...], g_ref[...], sh_ref[...], sc_ref[...]).astype(BF16)


def _modulate(x, g3, layer, shift, scale, rows_per_seq):
    m, d = x.shape
    tm = _pick(min(m, rows_per_seq), (512, 256, 128, 64, 32, 16, 8))
    return pl.pallas_call(
        _modulate_kernel,
        grid=(m // tm,),
        in_specs=[pl.BlockSpec((tm, d), lambda i: (i, 0)),
                  pl.BlockSpec((None, 1, d), lambda i: (layer, 0, 0)),
                  _mod_spec(shift, tm, d, rows_per_seq, 0, None),
                  _mod_spec(scale, tm, d, rows_per_seq, 0, None)],
        out_specs=pl.BlockSpec((tm, d), lambda i: (i, 0)),
        out_shape=jax.ShapeDtypeStruct((m, d), BF16),
        compiler_params=_params(("arbitrary",)),
        name="modulate",
    )(x, g3, shift, scale)


def _mm_kernel(*refs, epi, has_mul, n_out, scale):
    it = iter(refs)
    a_ref = next(it)
    m_ref = next(it) if has_mul else None
    w_ref = next(it)
    extra = []
    n_extra = {"headnorm": 1, "plain": 0, "logsig": 1, "sigmoid": 0, "resid": 2, "swiglu": 0}[epi]
    for _ in range(n_extra):
        extra.append(next(it))
    outs = [next(it) for _ in range(n_out)]
    wb_ref = next(it)

    @pl.when(pl.program_id(1) == 0)
    def _():
        wb_ref[...] = w_ref[...].astype(BF16)

    a = a_ref[...]
    if has_mul:
        a = a * m_ref[...]
    acc = _dot(a, wb_ref[...])

    if epi == "headnorm":
        g = extra[0][...]
        hd = g.shape[-1]
        for h in range(acc.shape[1] // hd):
            blk = acc[:, h * hd:(h + 1) * hd]
            ms = jnp.mean(blk * blk, axis=-1, keepdims=True)
            y = blk * lax.rsqrt(ms + NORM_EPS) * g
            for o in outs:
                if o.dtype == BF16:
                    o[:, h * hd:(h + 1) * hd] = (y * scale).astype(BF16)
                else:
                    o[:, h * hd:(h + 1) * hd] = y
        return
    if epi == "plain":
        res = acc
    elif epi == "logsig":
        z = acc + extra[0][...]
        res = jnp.minimum(z, 0.0) - jnp.log1p(jnp.exp(-jnp.abs(z)))
    elif epi == "sigmoid":
        res = jax.nn.sigmoid(acc)
    elif epi == "resid":
        res = extra[0][...] + extra[1][...] * acc
    elif epi == "swiglu":
        f = acc.shape[1] // 2
        ga = acc[:, :f]
        res = ga * jax.nn.sigmoid(ga) * acc[:, f:]
    for o in outs:
        o[...] = res.astype(o.dtype)


def _mm(a, w, layer, col0, n, *, epi, out_dtypes, mul=None, g=None, bias=None, xres=None,
        gate=None, rows_per_seq=None, scale=1.0, name="mm"):
    m, k = a.shape
    if epi == "swiglu":
        tn = n
    else:
        tn = _pick(n, (1024, 512, 256, 128))
    assert col0 % tn == 0
    cb = col0 // tn
    tm = _pick(m, (512, 256, 128))
    if rows_per_seq is not None:
        assert rows_per_seq % tm == 0 or (gate is not None and gate.ndim == 2)
    n_out_cols = n // 2 if epi == "swiglu" else n
    tno = tn // 2 if epi == "swiglu" else tn

    args = [a]
    specs = [pl.BlockSpec((tm, k), lambda j, i: (i, 0))]
    if mul is not None:
        args.append(mul)
        specs.append(pl.BlockSpec((tm, k), lambda j, i: (i, 0)))
    args.append(w)
    specs.append(pl.BlockSpec((None, k, tn), lambda j, i: (layer, 0, cb + j)))
    if epi == "headnorm":
        args.append(g)
        specs.append(pl.BlockSpec((None, 1, g.shape[-1]), lambda j, i: (layer, 0, 0)))
    elif epi == "logsig":
        args.append(bias)
        specs.append(pl.BlockSpec((None, 1, tn), lambda j, i: (layer, 0, j)))
    elif epi == "resid":
        args.append(xres)
        specs.append(pl.BlockSpec((tm, tn), lambda j, i: (i, j)))
        args.append(gate)
        specs.append(_mod_spec(gate, tm, tn, rows_per_seq, 1, 0))
    out_shape = [jax.ShapeDtypeStruct((m, n_out_cols), dt) for dt in out_dtypes]
    out_specs = [pl.BlockSpec((tm, tno), lambda j, i: (i, j)) for _ in out_dtypes]
    res = pl.pallas_call(
        functools.partial(_mm_kernel, epi=epi, has_mul=mul is not None, n_out=len(out_dtypes),
                          scale=scale),
        grid=(n // tn, m // tm),
        in_specs=specs,
        out_specs=out_specs,
        out_shape=out_shape,
        scratch_shapes=[pltpu.VMEM((k, tn), BF16)],
        compiler_params=_params(("arbitrary", "arbitrary")),
        name=name,
    )(*args)
    return res


def _chunk_prompt_kernel(q_ref, k_ref, v_ref, b_ref, o_ref, *, n_tiles, edge):
    def body(m, carry):
        q0 = pl.multiple_of(m * Q_TILE, Q_TILE)
        k0 = pl.multiple_of(jnp.maximum(m - edge, 0) * Q_TILE, Q_TILE)
        q = q_ref[pl.ds(q0, Q_TILE), :]
        k = k_ref[pl.ds(k0, BAND_KEYS), :]
        v = v_ref[pl.ds(k0, BAND_KEYS), :]
        s = _nt(q, k) + b_ref[jnp.minimum(m, edge)]
        mx = jnp.max(s, axis=-1, keepdims=True)
        p = jnp.exp(s - mx)
        l = jnp.sum(p, axis=-1, keepdims=True)
        o = _dot(p.astype(BF16), v) / l
        o_ref[pl.ds(q0, Q_TILE), :] = o.astype(BF16)
        return carry
    lax.fori_loop(0, n_tiles, body, 0)


def _chunk_bias_prompt(table):
    rel_max = (table.shape[1] - 1) // 2
    edge = (BAND_KEYS - Q_TILE) // Q_TILE
    mm = np.arange(edge + 1)[:, None, None]
    qpos = mm * Q_TILE + np.arange(Q_TILE)[None, :, None]
    kpos = np.maximum(mm - edge, 0) * Q_TILE + np.arange(BAND_KEYS)[None, None, :]
    qc, kc = qpos // CHUNK, kpos // CHUNK
    vis = (kc <= qc) & (kc >= qc - N_PREV_CHUNKS)
    idx = np.clip(qpos - kpos, -rel_max, rel_max) + rel_max
    b = table[:, idx].astype(F32)
    return jnp.where(vis[None], b, NEG_INF), edge


def _chunk_attn_prompt(q, k, v, bias, edge, bsz, seq, hd):
    d = q.shape[1]
    nh = d // hd
    q3, k3, v3 = (t.reshape(bsz, seq, d) for t in (q, k, v))
    spec = pl.BlockSpec((None, seq, hd), lambda b, h: (b, 0, h))
    o = pl.pallas_call(
        functools.partial(_chunk_prompt_kernel, n_tiles=seq // Q_TILE, edge=edge),
        grid=(bsz, nh),
        in_specs=[spec, spec, spec,
                  pl.BlockSpec((None, edge + 1, Q_TILE, BAND_KEYS), lambda b, h: (h, 0, 0, 0))],
        out_specs=spec,
        out_shape=jax.ShapeDtypeStruct((bsz, seq, d), BF16),
        compiler_params=_params(("arbitrary", "arbitrary")),
        name="chunk_attn_prompt",
    )(q3, k3, v3, bias)
    return o.reshape(bsz * seq, d)


def _sample_attn_kernel(*refs, hd, fox):
    if fox:
        q_ref, kn_ref, vn_ref, kc_ref, vc_ref, cq_ref, ckc_ref, ckn_ref, o_ref = refs
    else:
        q_ref, kn_ref, vn_ref, kc_ref, vc_ref, bc_ref, bn_ref, o_ref = refs
    t = q_ref.shape[0]
    nh = q_ref.shape[1] // hd
    if fox:
        row = lax.broadcasted_iota(jnp.int32, (t, t), 0)
        col = lax.broadcasted_iota(jnp.int32, (t, t), 1)
        causal = col <= row
    for h in range(nh):
        sl = slice(h * hd, (h + 1) * hd)
        q = q_ref[:, sl]
        kc = kc_ref[:, sl].astype(BF16)
        vc = vc_ref[:, sl].astype(BF16)
        s1 = _nt(q, kc)
        s2 = _nt(q, kn_ref[:, sl])
        if fox:
            cq = cq_ref[h]
            s1 = s1 + cq - ckc_ref[h]
            s2 = jnp.where(causal, s2 + cq - ckn_ref[h], NEG_INF)
        else:
            s1 = s1 + bc_ref[h]
            s2 = s2 + bn_ref[h]
        mx = jnp.maximum(jnp.max(s1, axis=-1, keepdims=True), jnp.max(s2, axis=-1, keepdims=True))
        p1 = jnp.exp(s1 - mx)
        p2 = jnp.exp(s2 - mx)
        l = jnp.sum(p1, axis=-1, keepdims=True) + jnp.sum(p2, axis=-1, keepdims=True)
        o = (_dot(p1.astype(BF16), vc) + _dot(p2.astype(BF16), vn_ref[:, sl])) / l
        o_ref[:, sl] = o.astype(BF16)


def _chunk_attn_sample(q, kn, vn, cache_k, cache_v, layer, bias_c, bias_n, t, hd):
    _, bsz, w, d = cache_k.shape
    nh = d // hd
    hg = _pick(nh, (8, 4, 2, 1)) * hd
    ng = d // hg
    tok = pl.BlockSpec((t, hg), lambda b, g: (b, g))
    cache = pl.BlockSpec((None, None, w, hg), lambda b, g: (layer, b, 0, g))
    return pl.pallas_call(
        functools.partial(_sample_attn_kernel, hd=hd, fox=False),
        grid=(bsz, ng),
        in_specs=[tok, tok, tok, cache, cache,
                  pl.BlockSpec((hg // hd, t, w), lambda b, g: (g, 0, 0)),
                  pl.BlockSpec((hg // hd, t, t), lambda b, g: (g, 0, 0))],
        out_specs=tok,
        out_shape=jax.ShapeDtypeStruct((bsz * t, d), BF16),
        compiler_params=_params(("arbitrary", "arbitrary")),
        name="chunk_attn_sample",
    )(q, kn, vn, cache_k, cache_v, bias_c, bias_n)


def _fox_attn_sample(q, kn, vn, cache_k, cache_v, layer, cq, ckc, ckn, t, hd):
    _, bsz, w, d = cache_k.shape
    nh = d // hd
    hg = _pick(nh, (4, 2, 1)) * hd
    ng = d // hg
    gh = hg // hd
    tok = pl.BlockSpec((t, hg), lambda b, g: (b, g))
    cache = pl.BlockSpec((None, None, w, hg), lambda b, g: (layer, b, 0, g))
    return pl.pallas_call(
        functools.partial(_sample_attn_kernel, hd=hd, fox=True),
        grid=(bsz, ng),
        in_specs=[tok, tok, tok, cache, cache,
                  pl.BlockSpec((None, gh, t, 1), lambda b, g: (b, g, 0, 0)),
                  pl.BlockSpec((None, gh, 1, w), lambda b, g: (b, g, 0, 0)),
                  pl.BlockSpec((None, gh, 1, t), lambda b, g: (b, g, 0, 0))],
        out_specs=tok,
        out_shape=jax.ShapeDtypeStruct((bsz * t, d), BF16),
        compiler_params=_params(("arbitrary", "arbitrary")),
        name="fox_attn_sample",
    )(q, kn, vn, cache_k, cache_v, cq, ckc, ckn)


def _fox_prompt_kernel(q_ref, k_ref, v_ref, cq_ref, ck_ref, o_ref):
    i = pl.program_id(2)
    tq = q_ref.shape[0]
    q = q_ref[...]
    cq = cq_ref[...]
    row = lax.broadcasted_iota(jnp.int32, (tq, tq), 0)
    col = lax.broadcasted_iota(jnp.int32, (tq, tq), 1)

    def step(j, carry, diag):
        m, l, acc = carry
        k0 = pl.multiple_of(j * tq, tq)
        k = k_ref[pl.ds(k0, tq), :]
        v = v_ref[pl.ds(k0, tq), :]
        s = _nt(q, k) + cq - ck_ref[:, pl.ds(k0, tq)]
        if diag:
            s = jnp.where(col <= row, s, NEG_INF)
        m_new = jnp.maximum(m, jnp.max(s, axis=-1, keepdims=True))
        alpha = jnp.exp(m - m_new)
        p = jnp.exp(s - m_new)
        l = alpha * l + jnp.sum(p, axis=-1, keepdims=True)
        acc = alpha * acc + _dot(p.astype(BF16), v)
        return m_new, l, acc

    init = (jnp.full((tq, 1), NEG_INF, F32), jnp.zeros((tq, 1), F32),
            jnp.zeros((tq, q_ref.shape[1]), F32))
    carry = lax.fori_loop(0, i, lambda j, c: step(j, c, False), init)
    m, l, acc = step(i, carry, True)
    o_ref[...] = (acc / l).astype(BF16)


def _fox_attn_prompt(q, k, v, cum_col, cum_row, bsz, seq, hd):
    d = q.shape[1]
    nh = d // hd
    tq = _pick(seq, (FOX_TILE, 128))
    q3, k3, v3 = (t.reshape(bsz, seq, d) for t in (q, k, v))
    qspec = pl.BlockSpec((None, tq, hd), lambda b, h, i: (b, i, h))
    kspec = pl.BlockSpec((None, seq, hd), lambda b, h, i: (b, 0, h))
    o = pl.pallas_call(
        _fox_prompt_kernel,
        grid=(bsz, nh, seq // tq),
        in_specs=[qspec, kspec, kspec,
                  pl.BlockSpec((None, None, tq, 1), lambda b, h, i: (b, h, i, 0)),
                  pl.BlockSpec((None, None, 1, seq), lambda b, h, i: (b, h, 0, 0))],
        out_specs=qspec,
        out_shape=jax.ShapeDtypeStruct((bsz, seq, d), BF16),
        compiler_params=_params(("arbitrary", "arbitrary", "arbitrary")),
        name="fox_attn_prompt",
    )(q3, k3, v3, cum_col, cum_row)
    return o.reshape(bsz * seq, d)


def _router_kernel(x_ref, g_ref, sh_ref, sc_ref, wt_ref, rb_ref, h_ref, ei_ref, ew_ref):
    hf = _modulated(x_ref[...], g_ref[...], sh_ref[...], sc_ref[...])
    hh = hf.astype(BF16)
    h_ref[...] = hh
    hl = (hf - hh.astype(F32)).astype(BF16)
    w = wt_ref[...]
    wh = w.astype(BF16)
    wl = (w - wh.astype(F32)).astype(BF16)
    logits = _nt(wh, hh) + (_nt(wh, hl) + _nt(wl, hh))
    scores = jax.nn.sigmoid(logits)
    sel = scores + rb_ref[...]
    ne, tm = sel.shape
    gsz = ne // N_GROUPS
    row_g = lax.broadcasted_iota(jnp.int32, (gsz, tm), 0).astype(F32)
    row_n = lax.broadcasted_iota(jnp.int32, (N_GROUPS, tm), 0).astype(F32)
    row_e = lax.broadcasted_iota(jnp.int32, (ne, tm), 0).astype(F32)

    def first_max(x, rows, big):
        mx = jnp.max(x, axis=0, keepdims=True)
        ix = jnp.min(jnp.where(x == mx, rows, big), axis=0, keepdims=True)
        return mx, ix

    grp = jnp.zeros((N_GROUPS, tm), F32)
    for g in range(N_GROUPS):
        blk = sel[g * gsz:(g + 1) * gsz, :]
        m1, i1 = first_max(blk, row_g, float(gsz))
        m2 = jnp.max(jnp.where(row_g == i1, REMOVED, blk), axis=0, keepdims=True)
        grp = jnp.where(row_n == float(g), m1 + m2, grp)
    gsel = jnp.zeros((N_GROUPS, tm), F32)
    work = grp
    for _ in range(TOPK_GROUPS):
        _, ix = first_max(work, row_n, float(N_GROUPS))
        hit = row_n == ix
        gsel = jnp.where(hit, 1.0, gsel)
        work = jnp.where(hit, REMOVED, work)
    work = jnp.concatenate(
        [jnp.where(gsel[g:g + 1, :] > 0.0, sel[g * gsz:(g + 1) * gsz, :], NEG_INF)
         for g in range(N_GROUPS)], axis=0)
    row_k = lax.broadcasted_iota(jnp.int32, (TOP_K, tm), 0)
    ei = jnp.zeros((TOP_K, tm), F32)
    ew = jnp.zeros((TOP_K, tm), F32)
    for r in range(TOP_K):
        _, ix = first_max(work, row_e, float(ne))
        hit = row_e == ix
        wr = jnp.sum(jnp.where(hit, scores, 0.0), axis=0, keepdims=True)
        ei = jnp.where(row_k == r, ix, ei)
        ew = jnp.where(row_k == r, wr, ew)
        work = jnp.where(hit, REMOVED, work)
    ew = ew / jnp.sum(ew, axis=0, keepdims=True) * ROUTED_SCALE
    ei_ref[...] = ei.astype(jnp.int32)
    ew_ref[...] = ew


def _router(x, g3, layer, shift, scale, rows_per_seq, router_wt, router_b3):
    m, d = x.shape
    ne = router_wt.shape[1]
    tm = _pick(m, (256, 128))
    return pl.pallas_call(
        _router_kernel,
        grid=(m // tm,),
        in_specs=[pl.BlockSpec((tm, d), lambda i: (i, 0)),
                  pl.BlockSpec((None, 1, d), lambda i: (layer, 0, 0)),
                  _mod_spec(shift, tm, d, rows_per_seq, 0, None),
                  _mod_spec(scale, tm, d, rows_per_seq, 0, None),
                  pl.BlockSpec((None, ne, d), lambda i: (layer, 0, 0)),
                  pl.BlockSpec((None, ne, 1), lambda i: (layer, 0, 0))],
        out_specs=[pl.BlockSpec((tm, d), lambda i: (i, 0)),
                   pl.BlockSpec((TOP_K, tm), lambda i: (0, i)),
                   pl.BlockSpec((TOP_K, tm), lambda i: (0, i))],
        out_shape=[jax.ShapeDtypeStruct((m, d), BF16),
                   jax.ShapeDtypeStruct((TOP_K, m), jnp.int32),
                   jax.ShapeDtypeStruct((TOP_K, m), F32)],
        compiler_params=_params(("arbitrary",)),
        name="router",
    )(x, g3, shift, scale, router_wt, router_b3)


def _experts_kernel(te_ref, nv_ref, x_ref, wg_ref, wu_ref, wd_ref, y_ref, wgb, wub, wdb):
    t = pl.program_id(0)
    e = te_ref[t]
    prev = te_ref[jnp.maximum(t - 1, 0)]

    @pl.when((t == 0) | (e != prev))
    def _():
        wgb[...] = wg_ref[...].astype(BF16)
        wub[...] = wu_ref[...].astype(BF16)
        wdb[...] = wd_ref[...].astype(BF16)

    @pl.when(t < nv_ref[0])
    def _():
        x = x_ref[...]
        a = _dot(x, wgb[...])
        b = _dot(x, wub[...])
        s = (a * jax.nn.sigmoid(a) * b).astype(BF16)
        y_ref[...] = _dot(s, wdb[...]).astype(BF16)


def _experts(xs, tile_expert, n_valid, wg, wu, wd, layer):
    s, d = xs.shape
    f = wg.shape[-1]
    tm = MOE_TILE
    n_tiles = s // tm

    def row_map(t, te, nv):
        return (jnp.minimum(t, nv[0] - 1), 0)
    grid_spec = pltpu.PrefetchScalarGridSpec(
        num_scalar_prefetch=2,
        grid=(n_tiles,),
        in_specs=[pl.BlockSpec((tm, d), row_map),
                  pl.BlockSpec((None, None, d, f), lambda t, te, nv: (layer, te[t], 0, 0)),
                  pl.BlockSpec((None, None, d, f), lambda t, te, nv: (layer, te[t], 0, 0)),
                  pl.BlockSpec((None, None, f, d), lambda t, te, nv: (layer, te[t], 0, 0))],
        out_specs=pl.BlockSpec((tm, d), row_map),
        scratch_shapes=[pltpu.VMEM((d, f), BF16), pltpu.VMEM((d, f), BF16), pltpu.VMEM((f, d), BF16)],
    )
    return pl.pallas_call(
        _experts_kernel,
        grid_spec=grid_spec,
        out_shape=jax.ShapeDtypeStruct((s, d), BF16),
        compiler_params=_params(("arbitrary",)),
        name="experts",
    )(tile_expert, n_valid, xs, wg, wu, wd)


def _moe_plan(ei, n_experts, tm):
    kk, t = ei.shape
    n_asg = kk * t
    n_tiles = n_asg // tm + n_experts
    flat = ei.reshape(-1)
    onehot = (flat[:, None] == jnp.arange(n_experts, dtype=jnp.int32)[None, :]).astype(jnp.int32)
    csum = jnp.cumsum(onehot, axis=0)
    gs = csum[-1]
    rank = jnp.sum((csum - onehot) * onehot, axis=1)
    tiles_per = (gs + tm - 1) // tm
    tile_end = jnp.cumsum(tiles_per)
    pstart = (tile_end - tiles_per) * tm
    ustart = jnp.cumsum(gs) - gs
    pos = pstart[flat] + rank
    order = jnp.argsort(flat, stable=True)
    n_valid = tile_end[-1]
    tile_ids = jnp.arange(n_tiles, dtype=jnp.int32)
    te = jnp.searchsorted(tile_end, tile_ids, side="right").astype(jnp.int32)
    te = jnp.minimum(te, te[jnp.maximum(n_valid - 1, 0)])
    slots = jnp.arange(n_tiles * tm, dtype=jnp.int32)
    se = te[slots // tm]
    within = slots - pstart[se]
    valid = (within < gs[se]) & (slots // tm < n_valid)
    src = order[jnp.clip(ustart[se] + within, 0, n_asg - 1)] % t
    slot_token = jnp.where(valid, src, 0).astype(jnp.int32)
    return slot_token, pos.reshape(kk, t).astype(jnp.int32), te, n_valid.reshape(1).astype(jnp.int32)


def _combine_kernel(x_ref, yg_ref, w_ref, sa_ref, sd_ref, gate_ref, o_ref, sdb):
    @pl.when(pl.program_id(0) == 0)
    def _():
        sdb[...] = sd_ref[...].astype(BF16)

    d = x_ref.shape[1]
    w = w_ref[...]
    acc = _dot(sa_ref[...], sdb[...])
    for k in range(TOP_K):
        acc = acc + w[:, k:k + 1] * yg_ref[:, k * d:(k + 1) * d].astype(F32)
    o_ref[...] = x_ref[...] + gate_ref[...] * acc


def _combine(x, yg, w, sact, sh_w_down, layer, gate, rows_per_seq):
    m, d = x.shape
    f = sact.shape[1]
    tm = _pick(m, (256, 128))
    return pl.pallas_call(
        _combine_kernel,
        grid=(m // tm,),
        in_specs=[pl.BlockSpec((tm, d), lambda i: (i, 0)),
                  pl.BlockSpec((tm, TOP_K * d), lambda i: (i, 0)),
                  pl.BlockSpec((tm, TOP_K), lambda i: (i, 0)),
                  pl.BlockSpec((tm, f), lambda i: (i, 0)),
                  pl.BlockSpec((None, f, d), lambda i: (layer, 0, 0)),
                  _mod_spec(gate, tm, d, rows_per_seq, 0, None)],
        out_specs=pl.BlockSpec((tm, d), lambda i: (i, 0)),
        out_shape=jax.ShapeDtypeStruct((m, d), F32),
        scratch_shapes=[pltpu.VMEM((f, d), BF16)],
        compiler_params=_params(("arbitrary",)),
        name="combine",
    )(x, yg, w, sact, sh_w_down, gate)


def kernel(x_prompt, x_sample, c_prompt, c_sample, cache_a_k, cache_a_v, cache_b_k, cache_b_v,
           cache_b_logf, ada_w, ada_b, norm1_g, norm2_g, a_w_in, a_q_g, a_k_g, a_rel_table, a_w_out,
           b_w_in, b_f_bias, b_q_g, b_k_g, b_w_out, router_w, router_bias, exp_w_gate, exp_w_up,
           exp_w_down, sh_w_gate, sh_w_up, sh_w_down):
    bsz, seq, d = x_prompt.shape
    dbsz, dseq, _ = x_sample.shape
    depth = ada_w.shape[0]
    hd = a_q_g.shape[-1]
    nh = d // hd
    ne = router_w.shape[-1]
    mp, ms = bsz * seq, dbsz * dseq
    attn_scale = hd ** -0.5

    xp = x_prompt.reshape(mp, d)
    xs = x_sample.reshape(ms, d)
    c_all = jnp.concatenate([c_prompt, c_sample], axis=0)
    ada_b3 = ada_b.reshape(depth, 1, -1)
    n1g = norm1_g.reshape(depth, 1, d)
    n2g = norm2_g.reshape(depth, 1, d)
    router_wt = jnp.swapaxes(router_w, 1, 2)
    router_b3 = router_bias.reshape(depth, ne, 1)
    sh_gu = jnp.concatenate([sh_w_gate, sh_w_up], axis=-1)

    a_out = {n: [] for n in ("kp", "vp", "ks", "vs")}
    b_out = {n: [] for n in ("kp", "vp", "lp", "ks", "vs", "ls")}

    for i in range(depth):
        mods = _adaln(c_all, ada_w, ada_b3, i)
        mod6 = jnp.split(mods, 6, axis=-1)
        mod_p = [t[:bsz].reshape(bsz, 1, d) for t in mod6]
        mod_s = [jnp.repeat(t[bsz:], dseq, axis=0) for t in mod6]
        hp = _modulate(xp, n1g, i, mod_p[0], mod_p[1], seq)
        hs = _modulate(xs, n1g, i, mod_s[0], mod_s[1], ms)
        j = i // 2
        if i % 2 == 0:
            qg, kg = a_q_g.reshape(-1, 1, hd), a_k_g.reshape(-1, 1, hd)
            proj = {}
            for nm, h_in in (("p", hp), ("s", hs)):
                q, = _mm(h_in, a_w_in, j, 0, d, epi="headnorm", g=qg, scale=attn_scale,
                         out_dtypes=(BF16,), name="a_q")
                kf, kb = _mm(h_in, a_w_in, j, d, d, epi="headnorm", g=kg,
                             out_dtypes=(F32, BF16), name="a_k")
                vf, vb = _mm(h_in, a_w_in, j, 2 * d, d, epi="plain",
                             out_dtypes=(F32, BF16), name="a_v")
                proj[nm] = (q, kf, kb, vf, vb)
            q, kf, kb, vf, vb = proj["p"]
            bias_p, edge = _chunk_bias_prompt(a_rel_table[j])
            op = _chunk_attn_prompt(q, kb, vb, bias_p, edge, bsz, seq, hd)
            keep = min(N_PREV_CHUNKS * CHUNK, seq)
            a_out["kp"].append(kf.reshape(bsz, seq, nh, hd)[:, seq - keep:])
            a_out["vp"].append(vf.reshape(bsz, seq, nh, hd)[:, seq - keep:])
            q, kf, kb, vf, vb = proj["s"]
            w = cache_a_k.shape[2]
            rel_max = (a_rel_table.shape[-1] - 1) // 2
            rel = np.arange(dseq)[:, None] + w - np.arange(w + dseq)[None, :]
            bias_s = a_rel_table[j][:, np.clip(rel, -rel_max, rel_max) + rel_max].astype(F32)
            os_ = _chunk_attn_sample(q, kb, vb, cache_a_k.reshape(-1, dbsz, w, d),
                                     cache_a_v.reshape(-1, dbsz, w, d), j,
                                     bias_s[:, :, :w], bias_s[:, :, w:], dseq, hd)
            a_out["ks"].append(kf.reshape(dbsz, dseq, nh, hd))
            a_out["vs"].append(vf.reshape(dbsz, dseq, nh, hd))
            mul_p = mul_s = None
            w_out = a_w_out
        else:
            qg, kg = b_q_g.reshape(-1, 1, hd), b_k_g.reshape(-1, 1, hd)
            w_f = jnp.pad(b_w_in[j][:, 3 * d:3 * d + nh], ((0, 0), (0, LANE - nh)))[None]
            f_b = jnp.pad(b_f_bias[j], (0, LANE - nh)).reshape(1, 1, LANE)
            w_g = b_w_in[j][:, 3 * d + nh:][None]
            proj = {}
            for nm, h_in in (("p", hp), ("s", hs)):
                q, = _mm(h_in, b_w_in, j, 0, d, epi="headnorm", g=qg, scale=attn_scale,
                         out_dtypes=(BF16,), name="b_q")
                kf, kb = _mm(h_in, b_w_in, j, d, d, epi="headnorm", g=kg,
                             out_dtypes=(F32, BF16), name="b_k")
                vf, vb = _mm(h_in, b_w_in, j, 2 * d, d, epi="plain",
                             out_dtypes=(F32, BF16), name="b_v")
                lf, = _mm(h_in, w_f, 0, 0, LANE, epi="logsig", bias=f_b, out_dtypes=(F32,),
                          name="b_logf")
                gs, = _mm(h_in, w_g, 0, 0, d, epi="sigmoid", out_dtypes=(BF16,), name="b_gate")
                proj[nm] = (q, kf, kb, vf, vb, lf[:, :nh], gs)
            q, kf, kb, vf, vb, lf, mul_p = proj["p"]
            lf3 = lf.reshape(bsz, seq, nh)
            cum = jnp.cumsum(lf3, axis=1).transpose(0, 2, 1)
            op = _fox_attn_prompt(q, kb, vb, cum[..., None], cum[:, :, None, :], bsz, seq, hd)
            b_out["kp"].append(kf.reshape(bsz, seq, nh, hd))
            b_out["vp"].append(vf.reshape(bsz, seq, nh, hd))
            b_out["lp"].append(lf3)
            q, kf, kb, vf, vb, lf, mul_s = proj["s"]
            past = cache_b_k.shape[2]
            lf3 = lf.reshape(dbsz, dseq, nh)
            lf_all = jnp.concatenate([cache_b_logf[j].astype(F32), lf3], axis=1)
            cum = jnp.cumsum(lf_all, axis=1).transpose(0, 2, 1)
            os_ = _fox_attn_sample(q, kb, vb, cache_b_k.reshape(-1, dbsz, past, d),
                                   cache_b_v.reshape(-1, dbsz, past, d), j,
                                   cum[:, :, past:, None], cum[:, :, None, :past],
                                   cum[:, :, None, past:], dseq, hd)
            b_out["ks"].append(kf.reshape(dbsz, dseq, nh, hd))
            b_out["vs"].append(vf.reshape(dbsz, dseq, nh, hd))
            b_out["ls"].append(lf3)
            w_out = b_w_out
        xp, = _mm(op, w_out, j, 0, d, epi="resid", mul=mul_p, xres=xp, gate=mod_p[2],
                  rows_per_seq=seq, out_dtypes=(F32,), name="out_proj")
        xs, = _mm(os_, w_out, j, 0, d, epi="resid", mul=mul_s, xres=xs, gate=mod_s[2],
                  rows_per_seq=ms, out_dtypes=(F32,), name="out_proj")

        h2p, eip, ewp = _router(xp, n2g, i, mod_p[3], mod_p[4], seq, router_wt, router_b3)
        h2s, eis, ews = _router(xs, n2g, i, mod_s[3], mod_s[4], ms, router_wt, router_b3)
        h2 = jnp.concatenate([h2p, h2s], axis=0)
        ei = jnp.concatenate([eip, eis], axis=1)
        slot_token, pos, tile_expert, n_valid = _moe_plan(ei, ne, MOE_TILE)
        xg = jnp.take(h2, slot_token, axis=0)
        ys = _experts(xg, tile_expert, n_valid, exp_w_gate, exp_w_up, exp_w_down, i)
        sp, = _mm(h2p, sh_gu, i, 0, sh_gu.shape[-1], epi="swiglu", out_dtypes=(BF16,), name="shared_up")
        ss, = _mm(h2s, sh_gu, i, 0, sh_gu.shape[-1], epi="swiglu", out_dtypes=(BF16,), name="shared_up")
        ygp = jnp.take(ys, pos[:, :mp].T, axis=0).reshape(mp, TOP_K * d)
        ygs = jnp.take(ys, pos[:, mp:].T, axis=0).reshape(ms, TOP_K * d)
        xp = _combine(xp, ygp, ewp.T, sp, sh_w_down, i, mod_p[5], seq)
        xs = _combine(xs, ygs, ews.T, ss, sh_w_down, i, mod_s[5], ms)

    st = jnp.stack
    return (xp.reshape(bsz, seq, d), xs.reshape(dbsz, dseq, d),
            st(a_out["kp"]), st(a_out["vp"]), st(a_out["ks"]), st(a_out["vs"]),
            st(b_out["kp"]), st(b_out["vp"]), st(b_out["lp"]),
            st(b_out["ks"]), st(b_out["vs"]), st(b_out["ls"]))
```

```python
import functools

import jax
import jax.numpy as jnp
import numpy as np
from jax import lax
from jax.experimental import pallas as pl
from jax.experimental.pallas import tpu as pltpu

F32 = jnp.float32
BF16 = jnp.bfloat16

CHUNK = 64
N_PREV_CHUNKS = 8
TOP_K = 8
N_GROUPS = 8
TOPK_GROUPS = 4
ROUTED_SCALE = 2.5
NORM_EPS = 1e-6
NEG_INF = -1e30
REMOVED = -3e38
LANE = 128
VMEM_LIMIT = 56 * 1024 * 1024
Q_TILE = 128
BAND_KEYS = (N_PREV_CHUNKS + 2) * CHUNK
FOX_TILE = 256
MOE_TILE = 256
LOG2E = 1.4426950408889634
CHUNK_HEADS = 4
FOX_HEADS = 4


def _pick(n, cands):
    for c in cands:
        if n % c == 0:
            return c
    return n


def _params(sem):
    return pltpu.CompilerParams(dimension_semantics=sem, vmem_limit_bytes=VMEM_LIMIT)


def _nt(a, b):
    return lax.dot_general(a, b, (((1,), (1,)), ((), ())), preferred_element_type=F32)


def _dot(a, b):
    return jnp.dot(a, b, preferred_element_type=F32)


def _mod_spec(mod, tm, tn, rows_per_seq, row_arg, col_arg):
    if mod.ndim == 3:
        def imap(*g):
            col = 0 if col_arg is None else g[col_arg]
            return ((g[row_arg] * tm) // rows_per_seq, 0, col)
        return pl.BlockSpec((None, 1, tn), imap)

    def imap2(*g):
        col = 0 if col_arg is None else g[col_arg]
        return (g[row_arg], col)
    return pl.BlockSpec((tm, tn), imap2)


def _adaln_kernel(c_ref, w_ref, b_ref, o_ref):
    c = c_ref[...]
    a = (c * jax.nn.sigmoid(c)).astype(BF16)
    o_ref[...] = _dot(a, w_ref[...].astype(BF16)) + b_ref[...]


def _adaln(c_all, ada_w, ada_b3, layer):
    r, d = c_all.shape
    n = ada_w.shape[2]
    tn = _pick(n, (1024, 512, 256, 128))
    return pl.pallas_call(
        _adaln_kernel,
        grid=(n // tn,),
        in_specs=[pl.BlockSpec((r, d), lambda j: (0, 0)),
                  pl.BlockSpec((None, d, tn), lambda j: (layer, 0, j)),
                  pl.BlockSpec((None, 1, tn), lambda j: (layer, 0, j))],
        out_specs=pl.BlockSpec((r, tn), lambda j: (0, j)),
        out_shape=jax.ShapeDtypeStruct((r, n), F32),
        compiler_params=_params(("arbitrary",)),
        name="adaln",
    )(c_all, ada_w, ada_b3)


def _modulated(x, g, sh, sc):
    ms = jnp.mean(x * x, axis=-1, keepdims=True)
    y = x * lax.rsqrt(ms + NORM_EPS) * g
    return y * (1.0 + sc) + sh


def _modulate_kernel(x_ref, g_ref, sh_ref, sc_ref, o_ref):
    o_ref[...] = _modulated(x_ref[...], g_ref[...], sh_ref[...], sc_ref[...]).astype(BF16)


def _modulate(x, g3, layer, shift, scale, rows_per_seq):
    m, d = x.shape
    tm = _pick(min(m, rows_per_seq), (512, 256, 128, 64, 32, 16, 8))
    return pl.pallas_call(
        _modulate_kernel,
        grid=(m // tm,),
        in_specs=[pl.BlockSpec((tm, d), lambda i: (i, 0)),
                  pl.BlockSpec((None, 1, d), lambda i: (layer, 0, 0)),
                  _mod_spec(shift, tm, d, rows_per_seq, 0, None),
                  _mod_spec(scale, tm, d, rows_per_seq, 0, None)],
        out_specs=pl.BlockSpec((tm, d), lambda i: (i, 0)),
        out_shape=jax.ShapeDtypeStruct((m, d), BF16),
        compiler_params=_params(("arbitrary",)),
        name="modulate",
    )(x, g3, shift, scale)


def _mm_kernel(*refs, epi, has_mul, n_out, scale):
    it = iter(refs)
    a_ref = next(it)
    m_ref = next(it) if has_mul else None
    w_ref = next(it)
    extra = []
    n_extra = {"headnorm": 1, "plain": 0, "logsig": 1, "sigmoid": 0, "resid": 2, "swiglu": 0}[epi]
    for _ in range(n_extra):
        extra.append(next(it))
    outs = [next(it) for _ in range(n_out)]
    wb_ref = next(it)

    @pl.when(pl.program_id(1) == 0)
    def _():
        wb_ref[...] = w_ref[...].astype(BF16)

    a = a_ref[...]
    if has_mul:
        a = a * m_ref[...]
    acc = _dot(a, wb_ref[...])

    if epi == "headnorm":
        g = extra[0][...]
        hd = g.shape[-1]
        for h in range(acc.shape[1] // hd):
            blk = acc[:, h * hd:(h + 1) * hd]
            ms = jnp.mean(blk * blk, axis=-1, keepdims=True)
            y = blk * lax.rsqrt(ms + NORM_EPS) * g
            for o in outs:
                if o.dtype == BF16:
                    o[:, h * hd:(h + 1) * hd] = (y * scale).astype(BF16)
                else:
                    o[:, h * hd:(h + 1) * hd] = y
        return
    if epi == "plain":
        res = acc
    elif epi == "logsig":
        z = acc + extra[0][...]
        res = jnp.minimum(z, 0.0) - jnp.log1p(jnp.exp(-jnp.abs(z)))
    elif epi == "sigmoid":
        res = jax.nn.sigmoid(acc)
    elif epi == "resid":
        res = extra[0][...] + extra[1][...] * acc
    elif epi == "swiglu":
        f = acc.shape[1] // 2
        ga = acc[:, :f]
        res = ga * jax.nn.sigmoid(ga) * acc[:, f:]
    for o in outs:
        o[...] = res.astype(o.dtype)


def _mm(a, w, layer, col0, n, *, epi, out_dtypes, mul=None, g=None, bias=None, xres=None,
        gate=None, rows_per_seq=None, scale=1.0, name="mm"):
    m, k = a.shape
    if epi == "swiglu":
        tn = n
    else:
        tn = _pick(n, (1024, 512, 256, 128))
    assert col0 % tn == 0
    cb = col0 // tn
    tm = _pick(m, (512, 256, 128))
    if rows_per_seq is not None:
        assert rows_per_seq % tm == 0 or (gate is not None and gate.ndim == 2)
    n_out_cols = n // 2 if epi == "swiglu" else n
    tno = tn // 2 if epi == "swiglu" else tn

    args = [a]
    specs = [pl.BlockSpec((tm, k), lambda j, i: (i, 0))]
    if mul is not None:
        args.append(mul)
        specs.append(pl.BlockSpec((tm, k), lambda j, i: (i, 0)))
    args.append(w)
    specs.append(pl.BlockSpec((None, k, tn), lambda j, i: (layer, 0, cb + j)))
    if epi == "headnorm":
        args.append(g)
        specs.append(pl.BlockSpec((None, 1, g.shape[-1]), lambda j, i: (layer, 0, 0)))
    elif epi == "logsig":
        args.append(bias)
        specs.append(pl.BlockSpec((None, 1, tn), lambda j, i: (layer, 0, j)))
    elif epi == "resid":
        args.append(xres)
        specs.append(pl.BlockSpec((tm, tn), lambda j, i: (i, j)))
        args.append(gate)
        specs.append(_mod_spec(gate, tm, tn, rows_per_seq, 1, 0))
    out_shape = [jax.ShapeDtypeStruct((m, n_out_cols), dt) for dt in out_dtypes]
    out_specs = [pl.BlockSpec((tm, tno), lambda j, i: (i, j)) for _ in out_dtypes]
    res = pl.pallas_call(
        functools.partial(_mm_kernel, epi=epi, has_mul=mul is not None, n_out=len(out_dtypes),
                          scale=scale),
        grid=(n // tn, m // tm),
        in_specs=specs,
        out_specs=out_specs,
        out_shape=out_shape,
        scratch_shapes=[pltpu.VMEM((k, tn), BF16)],
        compiler_params=_params(("arbitrary", "arbitrary")),
        name=name,
    )(*args)
    return res


def _chunk_prompt_kernel(q_ref, k_ref, v_ref, b_ref, o_ref, *, n_tiles, edge, hd):
    heads = q_ref.shape[1] // hd

    def body(m, carry):
        q0 = pl.multiple_of(m * Q_TILE, Q_TILE)
        k0 = pl.multiple_of(jnp.maximum(m - edge, 0) * Q_TILE, Q_TILE)
        bi = jnp.minimum(m, edge)
        sls = [slice(g * hd, (g + 1) * hd) for g in range(heads)]
        ss = [_nt(q_ref[pl.ds(q0, Q_TILE), sl], k_ref[pl.ds(k0, BAND_KEYS), sl]) for sl in sls]
        ps, ls = [], []
        for g in range(heads):
            s = ss[g] + b_ref[g, bi]
            mx = jnp.max(s, axis=-1, keepdims=True)
            p = jnp.exp2(s - mx)
            ls.append(jnp.sum(p, axis=-1, keepdims=True))
            ps.append(p.astype(BF16))
        for g in range(heads):
            o = _dot(ps[g], v_ref[pl.ds(k0, BAND_KEYS), sls[g]]) / ls[g]
            o_ref[pl.ds(q0, Q_TILE), sls[g]] = o.astype(BF16)
        return carry
    lax.fori_loop(0, n_tiles, body, 0)


def _chunk_bias_prompt(table):
    rel_max = (table.shape[1] - 1) // 2
    edge = (BAND_KEYS - Q_TILE) // Q_TILE
    mm = np.arange(edge + 1)[:, None, None]
    qpos = mm * Q_TILE + np.arange(Q_TILE)[None, :, None]
    kpos = np.maximum(mm - edge, 0) * Q_TILE + np.arange(BAND_KEYS)[None, None, :]
    qc, kc = qpos // CHUNK, kpos // CHUNK
    vis = (kc <= qc) & (kc >= qc - N_PREV_CHUNKS)
    idx = np.clip(qpos - kpos, -rel_max, rel_max) + rel_max
    b = table[:, idx].astype(F32) * LOG2E
    return jnp.where(vis[None], b, NEG_INF), edge


def _chunk_attn_prompt(q, k, v, bias, edge, bsz, seq, hd):
    d = q.shape[1]
    nh = d // hd
    heads = _pick(nh, (CHUNK_HEADS, 2, 1))
    q3, k3, v3 = (t.reshape(bsz, seq, d) for t in (q, k, v))
    spec = pl.BlockSpec((None, seq, heads * hd), lambda b, h: (b, 0, h))
    o = pl.pallas_call(
        functools.partial(_chunk_prompt_kernel, n_tiles=seq // Q_TILE, edge=edge, hd=hd),
        grid=(bsz, nh // heads),
        in_specs=[spec, spec, spec,
                  pl.BlockSpec((heads, edge + 1, Q_TILE, BAND_KEYS), lambda b, h: (h, 0, 0, 0))],
        out_specs=spec,
        out_shape=jax.ShapeDtypeStruct((bsz, seq, d), BF16),
        compiler_params=_params(("arbitrary", "arbitrary")),
        name="chunk_attn_prompt",
    )(q3, k3, v3, bias)
    return o.reshape(bsz * seq, d)


def _sample_attn_kernel(*refs, hd, fox):
    if fox:
        q_ref, kn_ref, vn_ref, kc_ref, vc_ref, cq_ref, ckc_ref, ckn_ref, o_ref = refs
    else:
        q_ref, kn_ref, vn_ref, kc_ref, vc_ref, bc_ref, bn_ref, o_ref = refs
    t = q_ref.shape[0]
    nh = q_ref.shape[1] // hd
    if fox:
        row = lax.broadcasted_iota(jnp.int32, (t, t), 0)
        col = lax.broadcasted_iota(jnp.int32, (t, t), 1)
        causal = col <= row
    for h in range(nh):
        sl = slice(h * hd, (h + 1) * hd)
        q = q_ref[:, sl]
        kc = kc_ref[:, sl].astype(BF16)
        vc = vc_ref[:, sl].astype(BF16)
        s1 = _nt(q, kc)
        s2 = _nt(q, kn_ref[:, sl])
        if fox:
            cq = cq_ref[h]
            s1 = s1 + cq - ckc_ref[h]
            s2 = jnp.where(causal, s2 + cq - ckn_ref[h], NEG_INF)
        else:
            s1 = s1 + bc_ref[h]
            s2 = s2 + bn_ref[h]
        mx = jnp.maximum(jnp.max(s1, axis=-1, keepdims=True), jnp.max(s2, axis=-1, keepdims=True))
        p1 = jnp.exp2(s1 - mx)
        p2 = jnp.exp2(s2 - mx)
        l = jnp.sum(p1, axis=-1, keepdims=True) + jnp.sum(p2, axis=-1, keepdims=True)
        o = (_dot(p1.astype(BF16), vc) + _dot(p2.astype(BF16), vn_ref[:, sl])) / l
        o_ref[:, sl] = o.astype(BF16)


def _chunk_attn_sample(q, kn, vn, cache_k, cache_v, layer, bias_c, bias_n, t, hd):
    _, bsz, w, d = cache_k.shape
    nh = d // hd
    hg = _pick(nh, (8, 4, 2, 1)) * hd
    ng = d // hg
    tok = pl.BlockSpec((t, hg), lambda b, g: (b, g))
    cache = pl.BlockSpec((None, None, w, hg), lambda b, g: (layer, b, 0, g))
    return pl.pallas_call(
        functools.partial(_sample_attn_kernel, hd=hd, fox=False),
        grid=(bsz, ng),
        in_specs=[tok, tok, tok, cache, cache,
                  pl.BlockSpec((hg // hd, t, w), lambda b, g: (g, 0, 0)),
                  pl.BlockSpec((hg // hd, t, t), lambda b, g: (g, 0, 0))],
        out_specs=tok,
        out_shape=jax.ShapeDtypeStruct((bsz * t, d), BF16),
        compiler_params=_params(("arbitrary", "arbitrary")),
        name="chunk_attn_sample",
    )(q, kn, vn, cache_k, cache_v, bias_c, bias_n)


def _fox_attn_sample(q, kn, vn, cache_k, cache_v, layer, cq, ckc, ckn, t, hd):
    _, bsz, w, d = cache_k.shape
    nh = d // hd
    hg = _pick(nh, (4, 2, 1)) * hd
    ng = d // hg
    gh = hg // hd
    tok = pl.BlockSpec((t, hg), lambda b, g: (b, g))
    cache = pl.BlockSpec((None, None, w, hg), lambda b, g: (layer, b, 0, g))
    return pl.pallas_call(
        functools.partial(_sample_attn_kernel, hd=hd, fox=True),
        grid=(bsz, ng),
        in_specs=[tok, tok, tok, cache, cache,
                  pl.BlockSpec((None, gh, t, 1), lambda b, g: (b, g, 0, 0)),
                  pl.BlockSpec((None, gh, 1, w), lambda b, g: (b, g, 0, 0)),
                  pl.BlockSpec((None, gh, 1, t), lambda b, g: (b, g, 0, 0))],
        out_specs=tok,
        out_shape=jax.ShapeDtypeStruct((bsz * t, d), BF16),
        compiler_params=_params(("arbitrary", "arbitrary")),
        name="fox_attn_sample",
    )(q, kn, vn, cache_k, cache_v, cq, ckc, ckn)


def _fox_prompt_kernel(qt_ref, k_ref, vt_ref, cq_ref, ck_ref, o_ref, *scr):
    i = pl.program_id(2)
    heads, aw, tq = qt_ref.shape
    m_scr, l_scr, acc_scr = scr[:heads], scr[heads:2 * heads], scr[2 * heads:]
    row = lax.broadcasted_iota(jnp.int32, (tq, tq), 0)
    col = lax.broadcasted_iota(jnp.int32, (tq, tq), 1)
    for g in range(heads):
        m_scr[g][...] = jnp.full((1, tq), NEG_INF, F32)
        l_scr[g][...] = jnp.zeros((1, tq), F32)
        acc_scr[g][...] = jnp.zeros(acc_scr[g].shape, F32)

    def step(j, diag):
        k0 = pl.multiple_of(j * tq, tq)
        sts = [_dot(k_ref[pl.ds(k0, tq), g * aw:(g + 1) * aw], qt_ref[g]) for g in range(heads)]
        pts, alphas = [], []
        for g in range(heads):
            ck = ck_ref[g, pl.ds(k0, tq), :]
            st = sts[g] + cq_ref[g] - jnp.concatenate([ck] * (tq // LANE), axis=1)
            if diag:
                st = jnp.where(row <= col, st, NEG_INF)
            m_old = m_scr[g][...]
            m_new = jnp.maximum(m_old, jnp.max(st, axis=0, keepdims=True))
            alpha = jnp.exp2(m_old - m_new)
            pt = jnp.exp2(st - m_new)
            l_scr[g][...] = alpha * l_scr[g][...] + jnp.sum(pt, axis=0, keepdims=True)
            m_scr[g][...] = m_new
            pts.append(pt.astype(BF16))
            alphas.append(alpha)
        for g in range(heads):
            acc_scr[g][...] = alphas[g] * acc_scr[g][...] + _dot(vt_ref[g, :, pl.ds(k0, tq)], pts[g])

    def body(j, carry):
        step(j, False)
        return carry
    lax.fori_loop(0, i, body, 0)
    step(i, True)
    for g in range(heads):
        o_ref[g] = (acc_scr[g][...] / l_scr[g][...]).astype(BF16)


def _fox_attn_prompt(q, k, v, cum, bsz, seq, hd):
    d = q.shape[1]
    nh = d // hd
    heads = _pick(nh, (FOX_HEADS, 2, 1))
    tq = _pick(seq, (FOX_TILE, 128))
    qt = q.reshape(bsz, seq, nh, hd).transpose(0, 2, 3, 1)
    vt = v.reshape(bsz, seq, nh, hd).transpose(0, 2, 3, 1)
    cum_t = cum.transpose(0, 2, 1)
    cq = cum_t[:, :, None, :]
    ck = jnp.broadcast_to(cum_t[..., None], (bsz, nh, seq, LANE))
    ot = pl.pallas_call(
        _fox_prompt_kernel,
        grid=(bsz, nh // heads, seq // tq),
        in_specs=[pl.BlockSpec((None, heads, hd, tq), lambda b, h, i: (b, h, 0, i)),
                  pl.BlockSpec((None, seq, heads * hd), lambda b, h, i: (b, 0, h)),
                  pl.BlockSpec((None, heads, hd, seq), lambda b, h, i: (b, h, 0, 0)),
                  pl.BlockSpec((None, heads, 1, tq), lambda b, h, i: (b, h, 0, i)),
                  pl.BlockSpec((None, heads, seq, LANE), lambda b, h, i: (b, h, 0, 0))],
        out_specs=pl.BlockSpec((None, heads, hd, tq), lambda b, h, i: (b, h, 0, i)),
        out_shape=jax.ShapeDtypeStruct((bsz, nh, hd, seq), BF16),
        scratch_shapes=([pltpu.VMEM((1, tq), F32)] * (2 * heads)
                        + [pltpu.VMEM((hd, tq), F32)] * heads),
        compiler_params=_params(("arbitrary", "arbitrary", "arbitrary")),
        name="fox_attn_prompt",
    )(qt, k.reshape(bsz, seq, d), vt, cq, ck)
    return ot.transpose(0, 3, 1, 2).reshape(bsz * seq, d)


def _router_kernel(x_ref, g_ref, sh_ref, sc_ref, wt_ref, rb_ref, cin_ref,
                   h_ref, ei_ref, ew_ref, rk_ref, cout_ref, cnt):
    @pl.when(pl.program_id(0) == 0)
    def _():
        cnt[...] = cin_ref[...]

    hf = _modulated(x_ref[...], g_ref[...], sh_ref[...], sc_ref[...])
    hh = hf.astype(BF16)
    h_ref[...] = hh
    hl = (hf - hh.astype(F32)).astype(BF16)
    w = wt_ref[...]
    wh = w.astype(BF16)
    wl = (w - wh.astype(F32)).astype(BF16)
    logits = _nt(wh, hh) + (_nt(wh, hl) + _nt(wl, hh))
    scores = jax.nn.sigmoid(logits)
    sel = scores + rb_ref[...]
    ne, tm = sel.shape
    gsz = ne // N_GROUPS
    row_g = lax.broadcasted_iota(jnp.int32, (gsz, tm), 0).astype(F32)
    row_n = lax.broadcasted_iota(jnp.int32, (N_GROUPS, tm), 0).astype(F32)
    row_e = lax.broadcasted_iota(jnp.int32, (ne, tm), 0).astype(F32)

    def first_max(x, rows, big):
        mx = jnp.max(x, axis=0, keepdims=True)
        ix = jnp.min(jnp.where(x == mx, rows, big), axis=0, keepdims=True)
        return mx, ix

    grp = jnp.zeros((N_GROUPS, tm), F32)
    for g in range(N_GROUPS):
        blk = sel[g * gsz:(g + 1) * gsz, :]
        m1, i1 = first_max(blk, row_g, float(gsz))
        m2 = jnp.max(jnp.where(row_g == i1, REMOVED, blk), axis=0, keepdims=True)
        grp = jnp.where(row_n == float(g), m1 + m2, grp)
    gsel = jnp.zeros((N_GROUPS, tm), F32)
    work = grp
    for _ in range(TOPK_GROUPS):
        _, ix = first_max(work, row_n, float(N_GROUPS))
        hit = row_n == ix
        gsel = jnp.where(hit, 1.0, gsel)
        work = jnp.where(hit, REMOVED, work)
    work = jnp.concatenate(
        [jnp.where(gsel[g:g + 1, :] > 0.0, sel[g * gsz:(g + 1) * gsz, :], NEG_INF)
         for g in range(N_GROUPS)], axis=0)
    row_k = lax.broadcasted_iota(jnp.int32, (TOP_K, tm), 0)
    ei = jnp.zeros((TOP_K, tm), F32)
    ew = jnp.zeros((TOP_K, tm), F32)
    hits = []
    chosen = jnp.zeros((ne, tm), F32)
    for r in range(TOP_K):
        _, ix = first_max(work, row_e, float(ne))
        hit = row_e == ix
        hits.append(hit)
        chosen = jnp.where(hit, 1.0, chosen)
        wr = jnp.sum(jnp.where(hit, scores, 0.0), axis=0, keepdims=True)
        ei = jnp.where(row_k == r, ix, ei)
        ew = jnp.where(row_k == r, wr, ew)
        work = jnp.where(hit, REMOVED, work)
    ew = ew / jnp.sum(ew, axis=0, keepdims=True) * ROUTED_SCALE
    ei_ref[...] = ei.astype(jnp.int32)
    ew_ref[...] = ew
    cb = chosen.astype(BF16)
    tr = lax.broadcasted_iota(jnp.int32, (tm, tm), 0)
    tc = lax.broadcasted_iota(jnp.int32, (tm, tm), 1)
    before = _dot(cb, jnp.where(tr < tc, 1.0, 0.0).astype(BF16))
    total = _dot(cb, jnp.ones((tm, LANE), BF16))
    base = cnt[...]
    rank_e = jnp.concatenate([base] * (tm // LANE), axis=1) + before
    rk = jnp.zeros((TOP_K, tm), F32)
    for r in range(TOP_K):
        rr = jnp.sum(jnp.where(hits[r], rank_e, 0.0), axis=0, keepdims=True)
        rk = jnp.where(row_k == r, rr, rk)
    rk_ref[...] = rk.astype(jnp.int32)
    cnt[...] = base + total
    cout_ref[...] = base + total


def _router(x, g3, layer, shift, scale, rows_per_seq, router_wt, router_b3, cnt_in):
    m, d = x.shape
    ne = router_wt.shape[1]
    tm = _pick(m, (256, 128))
    asg = pl.BlockSpec((TOP_K, tm), lambda i: (0, i))
    return pl.pallas_call(
        _router_kernel,
        grid=(m // tm,),
        in_specs=[pl.BlockSpec((tm, d), lambda i: (i, 0)),
                  pl.BlockSpec((None, 1, d), lambda i: (layer, 0, 0)),
                  _mod_spec(shift, tm, d, rows_per_seq, 0, None),
                  _mod_spec(scale, tm, d, rows_per_seq, 0, None),
                  pl.BlockSpec((None, ne, d), lambda i: (layer, 0, 0)),
                  pl.BlockSpec((None, ne, 1), lambda i: (layer, 0, 0)),
                  pl.BlockSpec((ne, LANE), lambda i: (0, 0))],
        out_specs=[pl.BlockSpec((tm, d), lambda i: (i, 0)), asg, asg, asg,
                   pl.BlockSpec((ne, LANE), lambda i: (0, 0))],
        out_shape=[jax.ShapeDtypeStruct((m, d), BF16),
                   jax.ShapeDtypeStruct((TOP_K, m), jnp.int32),
                   jax.ShapeDtypeStruct((TOP_K, m), F32),
                   jax.ShapeDtypeStruct((TOP_K, m), jnp.int32),
                   jax.ShapeDtypeStruct((ne, LANE), F32)],
        scratch_shapes=[pltpu.VMEM((ne, LANE), F32)],
        compiler_params=_params(("arbitrary",)),
        name="router",
    )(x, g3, shift, scale, router_wt, router_b3, cnt_in)


def _experts_kernel(te_ref, nv_ref, x_ref, wg_ref, wu_ref, wd_ref, y_ref, wgb, wub, wdb):
    t = pl.program_id(0)
    e = te_ref[t]
    prev = te_ref[jnp.maximum(t - 1, 0)]

    @pl.when((t == 0) | (e != prev))
    def _():
        wgb[...] = wg_ref[...].astype(BF16)
        wub[...] = wu_ref[...].astype(BF16)
        wdb[...] = wd_ref[...].astype(BF16)

    @pl.when(t < nv_ref[0])
    def _():
        x = x_ref[...]
        a = _dot(x, wgb[...])
        b = _dot(x, wub[...])
        s = (a * jax.nn.sigmoid(a) * b).astype(BF16)
        y_ref[...] = _dot(s, wdb[...]).astype(BF16)


def _experts(xs, tile_expert, n_valid, wg, wu, wd, layer):
    s, d = xs.shape
    f = wg.shape[-1]
    tm = MOE_TILE
    n_tiles = s // tm

    def row_map(t, te, nv):
        return (jnp.minimum(t, nv[0] - 1), 0)
    grid_spec = pltpu.PrefetchScalarGridSpec(
        num_scalar_prefetch=2,
        grid=(n_tiles,),
        in_specs=[pl.BlockSpec((tm, d), row_map),
                  pl.BlockSpec((None, None, d, f), lambda t, te, nv: (layer, te[t], 0, 0)),
                  pl.BlockSpec((None, None, d, f), lambda t, te, nv: (layer, te[t], 0, 0)),
                  pl.BlockSpec((None, None, f, d), lambda t, te, nv: (layer, te[t], 0, 0))],
        out_specs=pl.BlockSpec((tm, d), row_map),
        scratch_shapes=[pltpu.VMEM((d, f), BF16), pltpu.VMEM((d, f), BF16), pltpu.VMEM((f, d), BF16)],
    )
    return pl.pallas_call(
        _experts_kernel,
        grid_spec=grid_spec,
        out_shape=jax.ShapeDtypeStruct((s, d), BF16),
        compiler_params=_params(("arbitrary",)),
        name="experts",
    )(tile_expert, n_valid, xs, wg, wu, wd)


def _moe_plan(ei, rank, counts, tm):
    kk, t = ei.shape
    ne = counts.shape[0]
    n_asg = kk * t
    n_tiles = n_asg // tm + ne
    i32 = jnp.int32
    ar = jnp.arange(ne, dtype=i32)
    tiles_per = (counts + tm - 1) // tm
    tile_end = jnp.cumsum(tiles_per)
    pstart = (tile_end - tiles_per) * tm
    ustart = jnp.cumsum(counts) - counts
    pos = rank + jnp.sum(jnp.where(ei[..., None] == ar, pstart, 0), axis=-1)
    n_valid = tile_end[-1]
    tile_ids = jnp.arange(n_tiles, dtype=i32)
    te = jnp.sum((tile_end[None, :] <= tile_ids[:, None]).astype(i32), axis=1)
    te = jnp.minimum(te, jnp.sum((tile_end <= n_valid - 1).astype(i32)))
    onehot = te[:, None] == ar[None, :]
    t_pstart, t_count, t_ustart = (jnp.sum(jnp.where(onehot, v, 0), axis=1)
                                   for v in (pstart, counts, ustart))
    within = tile_ids[:, None] * tm + jnp.arange(tm, dtype=i32)[None, :] - t_pstart[:, None]
    valid = (within < t_count[:, None]) & (tile_ids[:, None] < n_valid)
    order = jnp.argsort(ei.T.reshape(-1), stable=True)
    src = jnp.take(order, jnp.clip(t_ustart[:, None] + within, 0, n_asg - 1).reshape(-1)) // kk
    slot_token = jnp.where(valid.reshape(-1), src, 0).astype(i32)
    return slot_token, pos.astype(i32), te.astype(i32), n_valid.reshape(1).astype(i32)


def _combine_kernel(x_ref, yg_ref, w_ref, sa_ref, sd_ref, gate_ref, o_ref, sdb):
    @pl.when(pl.program_id(0) == 0)
    def _():
        sdb[...] = sd_ref[...].astype(BF16)

    w = w_ref[...]
    acc = _dot(sa_ref[...], sdb[...])
    for k in range(TOP_K):
        acc = acc + w[:, k:k + 1] * yg_ref[k].astype(F32)
    o_ref[...] = x_ref[...] + gate_ref[...] * acc


def _combine(x, yg, w, sact, sh_w_down, layer, gate, rows_per_seq):
    m, d = x.shape
    f = sact.shape[1]
    tm = _pick(m, (256, 128))
    return pl.pallas_call(
        _combine_kernel,
        grid=(m // tm,),
        in_specs=[pl.BlockSpec((tm, d), lambda i: (i, 0)),
                  pl.BlockSpec((TOP_K, tm, d), lambda i: (0, i, 0)),
                  pl.BlockSpec((tm, TOP_K), lambda i: (i, 0)),
                  pl.BlockSpec((tm, f), lambda i: (i, 0)),
                  pl.BlockSpec((None, f, d), lambda i: (layer, 0, 0)),
                  _mod_spec(gate, tm, d, rows_per_seq, 0, None)],
        out_specs=pl.BlockSpec((tm, d), lambda i: (i, 0)),
        out_shape=jax.ShapeDtypeStruct((m, d), F32),
        scratch_shapes=[pltpu.VMEM((f, d), BF16)],
        compiler_params=_params(("arbitrary",)),
        name="combine",
    )(x, yg, w, sact, sh_w_down, gate)


def kernel(x_prompt, x_sample, c_prompt, c_sample, cache_a_k, cache_a_v, cache_b_k, cache_b_v,
           cache_b_logf, ada_w, ada_b, norm1_g, norm2_g, a_w_in, a_q_g, a_k_g, a_rel_table, a_w_out,
           b_w_in, b_f_bias, b_q_g, b_k_g, b_w_out, router_w, router_bias, exp_w_gate, exp_w_up,
           exp_w_down, sh_w_gate, sh_w_up, sh_w_down):
    bsz, seq, d = x_prompt.shape
    dbsz, dseq, _ = x_sample.shape
    depth = ada_w.shape[0]
    hd = a_q_g.shape[-1]
    nh = d // hd
    ne = router_w.shape[-1]
    mp, ms = bsz * seq, dbsz * dseq
    attn_scale = hd ** -0.5 * LOG2E

    xp = x_prompt.reshape(mp, d)
    xs = x_sample.reshape(ms, d)
    c_all = jnp.concatenate([c_prompt, c_sample], axis=0)
    ada_b3 = ada_b.reshape(depth, 1, -1)
    n1g = norm1_g.reshape(depth, 1, d)
    n2g = norm2_g.reshape(depth, 1, d)
    router_wt = jnp.swapaxes(router_w, 1, 2)
    router_b3 = router_bias.reshape(depth, ne, 1)
    sh_gu = jnp.concatenate([sh_w_gate, sh_w_up], axis=-1)

    a_out = {n: [] for n in ("kp", "vp", "ks", "vs")}
    b_out = {n: [] for n in ("kp", "vp", "lp", "ks", "vs", "ls")}

    for i in range(depth):
        mods = _adaln(c_all, ada_w, ada_b3, i)
        mod6 = jnp.split(mods, 6, axis=-1)
        mod_p = [t[:bsz].reshape(bsz, 1, d) for t in mod6]
        mod_s = [jnp.repeat(t[bsz:], dseq, axis=0) for t in mod6]
        hp = _modulate(xp, n1g, i, mod_p[0], mod_p[1], seq)
        hs = _modulate(xs, n1g, i, mod_s[0], mod_s[1], ms)
        j = i // 2
        if i % 2 == 0:
            qg, kg = a_q_g.reshape(-1, 1, hd), a_k_g.reshape(-1, 1, hd)
            proj = {}
            for nm, h_in in (("p", hp), ("s", hs)):
                q, = _mm(h_in, a_w_in, j, 0, d, epi="headnorm", g=qg, scale=attn_scale,
                         out_dtypes=(BF16,), name="a_q")
                kf, kb = _mm(h_in, a_w_in, j, d, d, epi="headnorm", g=kg,
                             out_dtypes=(F32, BF16), name="a_k")
                vf, vb = _mm(h_in, a_w_in, j, 2 * d, d, epi="plain",
                             out_dtypes=(F32, BF16), name="a_v")
                proj[nm] = (q, kf, kb, vf, vb)
            q, kf, kb, vf, vb = proj["p"]
            bias_p, edge = _chunk_bias_prompt(a_rel_table[j])
            op = _chunk_attn_prompt(q, kb, vb, bias_p, edge, bsz, seq, hd)
            keep = min(N_PREV_CHUNKS * CHUNK, seq)
            a_out["kp"].append(kf.reshape(bsz, seq, nh, hd)[:, seq - keep:])
            a_out["vp"].append(vf.reshape(bsz, seq, nh, hd)[:, seq - keep:])
            q, kf, kb, vf, vb = proj["s"]
            w = cache_a_k.shape[2]
            rel_max = (a_rel_table.shape[-1] - 1) // 2
            rel = np.arange(dseq)[:, None] + w - np.arange(w + dseq)[None, :]
            bias_s = a_rel_table[j][:, np.clip(rel, -rel_max, rel_max) + rel_max].astype(F32) * LOG2E
            os_ = _chunk_attn_sample(q, kb, vb, cache_a_k.reshape(-1, dbsz, w, d),
                                     cache_a_v.reshape(-1, dbsz, w, d), j,
                                     bias_s[:, :, :w], bias_s[:, :, w:], dseq, hd)
            a_out["ks"].append(kf.reshape(dbsz, dseq, nh, hd))
            a_out["vs"].append(vf.reshape(dbsz, dseq, nh, hd))
            mul_p = mul_s = None
            w_out = a_w_out
        else:
            qg, kg = b_q_g.reshape(-1, 1, hd), b_k_g.reshape(-1, 1, hd)
            w_f = jnp.pad(b_w_in[j][:, 3 * d:3 * d + nh], ((0, 0), (0, LANE - nh)))[None]
            f_b = jnp.pad(b_f_bias[j], (0, LANE - nh)).reshape(1, 1, LANE)
            w_g = b_w_in[j][:, 3 * d + nh:][None]
            proj = {}
            for nm, h_in in (("p", hp), ("s", hs)):
                q, = _mm(h_in, b_w_in, j, 0, d, epi="headnorm", g=qg, scale=attn_scale,
                         out_dtypes=(BF16,), name="b_q")
                kf, kb = _mm(h_in, b_w_in, j, d, d, epi="headnorm", g=kg,
                             out_dtypes=(F32, BF16), name="b_k")
                vf, vb = _mm(h_in, b_w_in, j, 2 * d, d, epi="plain",
                             out_dtypes=(F32, BF16), name="b_v")
                lf, = _mm(h_in, w_f, 0, 0, LANE, epi="logsig", bias=f_b, out_dtypes=(F32,),
                          name="b_logf")
                gs, = _mm(h_in, w_g, 0, 0, d, epi="sigmoid", out_dtypes=(BF16,), name="b_gate")
                proj[nm] = (q, kf, kb, vf, vb, lf[:, :nh], gs)
            q, kf, kb, vf, vb, lf, mul_p = proj["p"]
            lf3 = lf.reshape(bsz, seq, nh)
            op = _fox_attn_prompt(q, kb, vb, jnp.cumsum(lf3, axis=1) * LOG2E, bsz, seq, hd)
            b_out["kp"].append(kf.reshape(bsz, seq, nh, hd))
            b_out["vp"].append(vf.reshape(bsz, seq, nh, hd))
            b_out["lp"].append(lf3)
            q, kf, kb, vf, vb, lf, mul_s = proj["s"]
            past = cache_b_k.shape[2]
            lf3 = lf.reshape(dbsz, dseq, nh)
            lf_all = jnp.concatenate([cache_b_logf[j].astype(F32), lf3], axis=1)
            cum = jnp.cumsum(lf_all, axis=1).transpose(0, 2, 1) * LOG2E
            os_ = _fox_attn_sample(q, kb, vb, cache_b_k.reshape(-1, dbsz, past, d),
                                   cache_b_v.reshape(-1, dbsz, past, d), j,
                                   cum[:, :, past:, None], cum[:, :, None, :past],
                                   cum[:, :, None, past:], dseq, hd)
            b_out["ks"].append(kf.reshape(dbsz, dseq, nh, hd))
            b_out["vs"].append(vf.reshape(dbsz, dseq, nh, hd))
            b_out["ls"].append(lf3)
            w_out = b_w_out
        xp, = _mm(op, w_out, j, 0, d, epi="resid", mul=mul_p, xres=xp, gate=mod_p[2],
                  rows_per_seq=seq, out_dtypes=(F32,), name="out_proj")
        xs, = _mm(os_, w_out, j, 0, d, epi="resid", mul=mul_s, xres=xs, gate=mod_s[2],
                  rows_per_seq=ms, out_dtypes=(F32,), name="out_proj")

        cnt0 = jnp.zeros((ne, LANE), F32)
        h2p, eip, ewp, rkp, cnt1 = _router(xp, n2g, i, mod_p[3], mod_p[4], seq, router_wt, router_b3, cnt0)
        h2s, eis, ews, rks, cnt2 = _router(xs, n2g, i, mod_s[3], mod_s[4], ms, router_wt, router_b3, cnt1)
        h2 = jnp.concatenate([h2p, h2s], axis=0)
        ei = jnp.concatenate([eip, eis], axis=1)
        rank = jnp.concatenate([rkp, rks], axis=1)
        slot_token, pos, tile_expert, n_valid = _moe_plan(ei, rank, cnt2[:, 0].astype(jnp.int32), MOE_TILE)
        xg = jnp.take(h2, slot_token, axis=0)
        ys = _experts(xg, tile_expert, n_valid, exp_w_gate, exp_w_up, exp_w_down, i)
        sp, = _mm(h2p, sh_gu, i, 0, sh_gu.shape[-1], epi="swiglu", out_dtypes=(BF16,), name="shared_up")
        ss, = _mm(h2s, sh_gu, i, 0, sh_gu.shape[-1], epi="swiglu", out_dtypes=(BF16,), name="shared_up")
        ygp = jnp.take(ys, pos[:, :mp].reshape(-1), axis=0).reshape(TOP_K, mp, d)
        ygs = jnp.take(ys, pos[:, mp:].reshape(-1), axis=0).reshape(TOP_K, ms, d)
        xp = _combine(xp, ygp, ewp.T, sp, sh_w_down, i, mod_p[5], seq)
        xs = _combine(xs, ygs, ews.T, ss, sh_w_down, i, mod_s[5], ms)

    st = jnp.stack
    return (xp.reshape(bsz, seq, d), xs.reshape(dbsz, dseq, d),
            st(a_out["kp"]), st(a_out["vp"]), st(a_out["ks"]), st(a_out["vs"]),
            st(b_out["kp"]), st(b_out["vp"]), st(b_out["lp"]),
            st(b_out["ks"]), st(b_out["vs"]), st(b_out["ls"]))
```

```python
import functools

import jax
import jax.numpy as jnp
import numpy as np
from jax import lax
from jax.experimental import pallas as pl
from jax.experimental.pallas import tpu as pltpu

F32 = jnp.float32
BF16 = jnp.bfloat16

CHUNK = 64
N_PREV_CHUNKS = 8
TOP_K = 8
N_GROUPS = 8
TOPK_GROUPS = 4
ROUTED_SCALE = 2.5
NORM_EPS = 1e-6
NEG_INF = -1e30
REMOVED = -3e38
LANE = 128
VMEM_LIMIT = 56 * 1024 * 1024
Q_TILE = 128
BAND_KEYS = (N_PREV_CHUNKS + 2) * CHUNK
FOX_TILE = 256
MOE_TILE = 256
LOG2E = 1.4426950408889634
CHUNK_HEADS = 4
FOX_HEADS = 4


def _pick(n, cands):
    for c in cands:
        if n % c == 0:
            return c
    return n


def _params(sem):
    return pltpu.CompilerParams(dimension_semantics=sem, vmem_limit_bytes=VMEM_LIMIT)


def _nt(a, b):
    return lax.dot_general(a, b, (((1,), (1,)), ((), ())), preferred_element_type=F32)


def _dot(a, b):
    return jnp.dot(a, b, preferred_element_type=F32)


def _mod_spec(mod, tm, tn, rows_per_seq, row_arg, col_arg):
    if mod.ndim == 3:
        def imap(*g):
            col = 0 if col_arg is None else g[col_arg]
            return ((g[row_arg] * tm) // rows_per_seq, 0, col)
        return pl.BlockSpec((None, 1, tn), imap)

    def imap2(*g):
        col = 0 if col_arg is None else g[col_arg]
        return (g[row_arg], col)
    return pl.BlockSpec((tm, tn), imap2)


def _adaln_kernel(c_ref, w_ref, b_ref, o_ref):
    c = c_ref[...]
    a = (c * jax.nn.sigmoid(c)).astype(BF16)
    o_ref[...] = _dot(a, w_ref[...].astype(BF16)) + b_ref[...]


def _adaln(c_all, ada_w, ada_b3, layer):
    r, d = c_all.shape
    n = ada_w.shape[2]
    tn = _pick(n, (1024, 512, 256, 128))
    return pl.pallas_call(
        _adaln_kernel,
        grid=(n // tn,),
        in_specs=[pl.BlockSpec((r, d), lambda j: (0, 0)),
                  pl.BlockSpec((None, d, tn), lambda j: (layer, 0, j)),
                  pl.BlockSpec((None, 1, tn), lambda j: (layer, 0, j))],
        out_specs=pl.BlockSpec((r, tn), lambda j: (0, j)),
        out_shape=jax.ShapeDtypeStruct((r, n), F32),
        compiler_params=_params(("arbitrary",)),
        name="adaln",
    )(c_all, ada_w, ada_b3)


def _modulated(x, g, sh, sc):
    ms = jnp.mean(x * x, axis=-1, keepdims=True)
    y = x * lax.rsqrt(ms + NORM_EPS) * g
    return y * (1.0 + sc) + sh


def _modulate_kernel(x_ref, g_ref, sh_ref, sc_ref, o_ref):
    o_ref[...] = _modulated(x_ref[...], g_ref[...], sh_ref[...], sc_ref[...]).astype(BF16)


def _modulate(x, g3, layer, shift, scale, rows_per_seq):
    m, d = x.shape
    tm = _pick(min(m, rows_per_seq), (512, 256, 128, 64, 32, 16, 8))
    return pl.pallas_call(
        _modulate_kernel,
        grid=(m // tm,),
        in_specs=[pl.BlockSpec((tm, d), lambda i: (i, 0)),
                  pl.BlockSpec((None, 1, d), lambda i: (layer, 0, 0)),
                  _mod_spec(shift, tm, d, rows_per_seq, 0, None),
                  _mod_spec(scale, tm, d, rows_per_seq, 0, None)],
        out_specs=pl.BlockSpec((tm, d), lambda i: (i, 0)),
        out_shape=jax.ShapeDtypeStruct((m, d), BF16),
        compiler_params=_params(("arbitrary",)),
        name="modulate",
    )(x, g3, shift, scale)


def _mm_kernel(*refs, epi, has_mul, n_out, scale):
    it = iter(refs)
    a_ref = next(it)
    m_ref = next(it) if has_mul else None
    w_ref = next(it)
    extra = []
    n_extra = {"headnorm": 1, "plain": 0, "logsig": 1, "sigmoid": 0, "resid": 2, "swiglu": 0}[epi]
    for _ in range(n_extra):
        extra.append(next(it))
    outs = [next(it) for _ in range(n_out)]
    wb_ref = next(it)

    @pl.when(pl.program_id(1) == 0)
    def _():
        wb_ref[...] = w_ref[...].astype(BF16)

    a = a_ref[...]
    if has_mul:
        a = a * m_ref[...]
    acc = _dot(a, wb_ref[...])

    if epi == "headnorm":
        g = extra[0][...]
        hd = g.shape[-1]
        ys = []
        for h in range(acc.shape[1] // hd):
            blk = acc[:, h * hd:(h + 1) * hd]
            ms = jnp.mean(blk * blk, axis=-1, keepdims=True)
            ys.append(blk * lax.rsqrt(ms + NORM_EPS) * g)
        res = jnp.concatenate(ys, axis=1)
    elif epi == "plain":
        res = acc
    elif epi == "logsig":
        z = acc + extra[0][...]
        res = jnp.minimum(z, 0.0) - jnp.log1p(jnp.exp(-jnp.abs(z)))
    elif epi == "sigmoid":
        res = jax.nn.sigmoid(acc)
    elif epi == "resid":
        res = extra[0][...] + extra[1][...] * acc
    elif epi == "swiglu":
        f = acc.shape[1] // 2
        ga = acc[:, :f]
        res = ga * jax.nn.sigmoid(ga) * acc[:, f:]
    for o in outs:
        val = res * scale if (o.dtype == BF16 and scale != 1.0) else res
        if len(o.shape) == 3:
            val = val.reshape(o.shape)
        o[...] = val.astype(o.dtype)


def _mm(a, w, layer, col0, n, *, epi, out_dtypes, mul=None, g=None, bias=None, xres=None,
        gate=None, rows_per_seq=None, scale=1.0, head_dim=None, name="mm"):
    m, k = a.shape
    if epi == "swiglu":
        tn = n
    else:
        tn = _pick(n, (1024, 512, 256, 128))
    assert col0 % tn == 0
    cb = col0 // tn
    tm = _pick(m, (512, 256, 128))
    if rows_per_seq is not None:
        assert rows_per_seq % tm == 0 or (gate is not None and gate.ndim == 2)
    n_out_cols = n // 2 if epi == "swiglu" else n
    tno = tn // 2 if epi == "swiglu" else tn

    args = [a]
    specs = [pl.BlockSpec((tm, k), lambda j, i: (i, 0))]
    if mul is not None:
        args.append(mul)
        specs.append(pl.BlockSpec((tm, k), lambda j, i: (i, 0)))
    args.append(w)
    specs.append(pl.BlockSpec((None, k, tn), lambda j, i: (layer, 0, cb + j)))
    if epi == "headnorm":
        args.append(g)
        specs.append(pl.BlockSpec((None, 1, g.shape[-1]), lambda j, i: (layer, 0, 0)))
    elif epi == "logsig":
        args.append(bias)
        specs.append(pl.BlockSpec((None, 1, tn), lambda j, i: (layer, 0, j)))
    elif epi == "resid":
        args.append(xres)
        specs.append(pl.BlockSpec((tm, tn), lambda j, i: (i, j)))
        args.append(gate)
        specs.append(_mod_spec(gate, tm, tn, rows_per_seq, 1, 0))
    out_shape, out_specs = [], []
    for dt in out_dtypes:
        if head_dim is not None and dt == F32:
            out_shape.append(jax.ShapeDtypeStruct((m, n_out_cols // head_dim, head_dim), dt))
            out_specs.append(pl.BlockSpec((tm, tno // head_dim, head_dim), lambda j, i: (i, j, 0)))
        else:
            out_shape.append(jax.ShapeDtypeStruct((m, n_out_cols), dt))
            out_specs.append(pl.BlockSpec((tm, tno), lambda j, i: (i, j)))
    res = pl.pallas_call(
        functools.partial(_mm_kernel, epi=epi, has_mul=mul is not None, n_out=len(out_dtypes),
                          scale=scale),
        grid=(n // tn, m // tm),
        in_specs=specs,
        out_specs=out_specs,
        out_shape=out_shape,
        scratch_shapes=[pltpu.VMEM((k, tn), BF16)],
        compiler_params=_params(("arbitrary", "arbitrary")),
        name=name,
    )(*args)
    return res


def _chunk_prompt_kernel(q_ref, k_ref, v_ref, b_ref, o_ref, *, n_tiles, edge, hd):
    heads = q_ref.shape[1] // hd

    def body(m, carry):
        q0 = pl.multiple_of(m * Q_TILE, Q_TILE)
        k0 = pl.multiple_of(jnp.maximum(m - edge, 0) * Q_TILE, Q_TILE)
        bi = jnp.minimum(m, edge)
        sls = [slice(g * hd, (g + 1) * hd) for g in range(heads)]
        ss = [_nt(q_ref[pl.ds(q0, Q_TILE), sl], k_ref[pl.ds(k0, BAND_KEYS), sl]) for sl in sls]
        ps, ls = [], []
        for g in range(heads):
            s = ss[g] + b_ref[g, bi]
            mx = jnp.max(s, axis=-1, keepdims=True)
            p = jnp.exp2(s - mx)
            ls.append(jnp.sum(p, axis=-1, keepdims=True))
            ps.append(p.astype(BF16))
        for g in range(heads):
            o = _dot(ps[g], v_ref[pl.ds(k0, BAND_KEYS), sls[g]]) / ls[g]
            o_ref[pl.ds(q0, Q_TILE), sls[g]] = o.astype(BF16)
        return carry
    lax.fori_loop(0, n_tiles, body, 0)


def _chunk_bias_prompt(table):
    nh = table.shape[0]
    rel_max = (table.shape[1] - 1) // 2
    edge = (BAND_KEYS - Q_TILE) // Q_TILE
    mm = np.arange(edge + 1)[:, None, None]
    qpos = mm * Q_TILE + np.arange(Q_TILE)[None, :, None]
    kpos = np.maximum(mm - edge, 0) * Q_TILE + np.arange(BAND_KEYS)[None, None, :]
    qc, kc = qpos // CHUNK, kpos // CHUNK
    vis = (kc <= qc) & (kc >= qc - N_PREV_CHUNKS)
    period = Q_TILE + BAND_KEYS - 1
    c_m = np.minimum(np.arange(edge + 1), edge) * Q_TILE
    n = np.arange(period)[None, :] - (BAND_KEYS - 1)
    idx = np.clip(c_m[:, None] + n, -rel_max, rel_max) + rel_max
    u = table[:, idx].astype(F32) * LOG2E
    skew = jnp.tile(u, (1, 1, Q_TILE + 1))[..., :Q_TILE * (period + 1)]
    skew = skew.reshape(nh, edge + 1, Q_TILE, period + 1)[..., :BAND_KEYS]
    b = skew[..., ::-1]
    return jnp.where(vis[None], b, NEG_INF), edge


def _chunk_attn_prompt(q, k, v, bias, edge, bsz, seq, hd):
    d = q.shape[1]
    nh = d // hd
    heads = _pick(nh, (CHUNK_HEADS, 2, 1))
    q3, k3, v3 = (t.reshape(bsz, seq, d) for t in (q, k, v))
    spec = pl.BlockSpec((None, seq, heads * hd), lambda b, h: (b, 0, h))
    o = pl.pallas_call(
        functools.partial(_chunk_prompt_kernel, n_tiles=seq // Q_TILE, edge=edge, hd=hd),
        grid=(bsz, nh // heads),
        in_specs=[spec, spec, spec,
                  pl.BlockSpec((heads, edge + 1, Q_TILE, BAND_KEYS), lambda b, h: (h, 0, 0, 0))],
        out_specs=spec,
        out_shape=jax.ShapeDtypeStruct((bsz, seq, d), BF16),
        compiler_params=_params(("arbitrary", "arbitrary")),
        name="chunk_attn_prompt",
    )(q3, k3, v3, bias)
    return o.reshape(bsz * seq, d)


def _sample_attn_kernel(*refs, hd, fox):
    if fox:
        q_ref, kn_ref, vn_ref, kc_ref, vc_ref, cq_ref, ckc_ref, ckn_ref, o_ref = refs
    else:
        q_ref, kn_ref, vn_ref, kc_ref, vc_ref, bc_ref, bn_ref, o_ref = refs
    t = q_ref.shape[0]
    nh = q_ref.shape[1] // hd
    if fox:
        row = lax.broadcasted_iota(jnp.int32, (t, t), 0)
        col = lax.broadcasted_iota(jnp.int32, (t, t), 1)
        causal = col <= row
    for h in range(nh):
        sl = slice(h * hd, (h + 1) * hd)
        q = q_ref[:, sl]
        kc = kc_ref[:, sl].astype(BF16)
        vc = vc_ref[:, sl].astype(BF16)
        s1 = _nt(q, kc)
        s2 = _nt(q, kn_ref[:, sl])
        if fox:
            cq = cq_ref[h]
            s1 = s1 + cq - ckc_ref[h]
            s2 = jnp.where(causal, s2 + cq - ckn_ref[h], NEG_INF)
        else:
            s1 = s1 + bc_ref[h]
            s2 = s2 + bn_ref[h]
        mx = jnp.maximum(jnp.max(s1, axis=-1, keepdims=True), jnp.max(s2, axis=-1, keepdims=True))
        p1 = jnp.exp2(s1 - mx)
        p2 = jnp.exp2(s2 - mx)
        l = jnp.sum(p1, axis=-1, keepdims=True) + jnp.sum(p2, axis=-1, keepdims=True)
        o = (_dot(p1.astype(BF16), vc) + _dot(p2.astype(BF16), vn_ref[:, sl])) / l
        o_ref[:, sl] = o.astype(BF16)


def _chunk_attn_sample(q, kn, vn, cache_k, cache_v, layer, bias_c, bias_n, t, hd):
    _, bsz, w, d = cache_k.shape
    nh = d // hd
    hg = _pick(nh, (8, 4, 2, 1)) * hd
    ng = d // hg
    tok = pl.BlockSpec((t, hg), lambda b, g: (b, g))
    cache = pl.BlockSpec((None, None, w, hg), lambda b, g: (layer, b, 0, g))
    return pl.pallas_call(
        functools.partial(_sample_attn_kernel, hd=hd, fox=False),
        grid=(bsz, ng),
        in_specs=[tok, tok, tok, cache, cache,
                  pl.BlockSpec((hg // hd, t, w), lambda b, g: (g, 0, 0)),
                  pl.BlockSpec((hg // hd, t, t), lambda b, g: (g, 0, 0))],
        out_specs=tok,
        out_shape=jax.ShapeDtypeStruct((bsz * t, d), BF16),
        compiler_params=_params(("arbitrary", "arbitrary")),
        name="chunk_attn_sample",
    )(q, kn, vn, cache_k, cache_v, bias_c, bias_n)


def _fox_attn_sample(q, kn, vn, cache_k, cache_v, layer, cq, ckc, ckn, t, hd):
    _, bsz, w, d = cache_k.shape
    nh = d // hd
    hg = _pick(nh, (4, 2, 1)) * hd
    ng = d // hg
    gh = hg // hd
    tok = pl.BlockSpec((t, hg), lambda b, g: (b, g))
    cache = pl.BlockSpec((None, None, w, hg), lambda b, g: (layer, b, 0, g))
    return pl.pallas_call(
        functools.partial(_sample_attn_kernel, hd=hd, fox=True),
        grid=(bsz, ng),
        in_specs=[tok, tok, tok, cache, cache,
                  pl.BlockSpec((None, gh, t, 1), lambda b, g: (b, g, 0, 0)),
                  pl.BlockSpec((None, gh, 1, w), lambda b, g: (b, g, 0, 0)),
                  pl.BlockSpec((None, gh, 1, t), lambda b, g: (b, g, 0, 0))],
        out_specs=tok,
        out_shape=jax.ShapeDtypeStruct((bsz * t, d), BF16),
        compiler_params=_params(("arbitrary", "arbitrary")),
        name="fox_attn_sample",
    )(q, kn, vn, cache_k, cache_v, cq, ckc, ckn)


def _fox_prompt_kernel(qt_ref, k_ref, vt_ref, cq_ref, ck_ref, o_ref, *scr):
    i = pl.program_id(2)
    heads, aw, tq = qt_ref.shape
    m_scr, l_scr, acc_scr = scr[:heads], scr[heads:2 * heads], scr[2 * heads:]
    row = lax.broadcasted_iota(jnp.int32, (tq, tq), 0)
    col = lax.broadcasted_iota(jnp.int32, (tq, tq), 1)
    for g in range(heads):
        m_scr[g][...] = jnp.full((1, tq), NEG_INF, F32)
        l_scr[g][...] = jnp.zeros((1, tq), F32)
        acc_scr[g][...] = jnp.zeros(acc_scr[g].shape, F32)

    def step(j, diag):
        k0 = pl.multiple_of(j * tq, tq)
        sts = [_dot(k_ref[pl.ds(k0, tq), g * aw:(g + 1) * aw], qt_ref[g]) for g in range(heads)]
        pts, alphas = [], []
        for g in range(heads):
            ck = ck_ref[g, pl.ds(k0, tq), :]
            st = sts[g] + cq_ref[g] - jnp.concatenate([ck] * (tq // LANE), axis=1)
            if diag:
                st = jnp.where(row <= col, st, NEG_INF)
            m_old = m_scr[g][...]
            m_new = jnp.maximum(m_old, jnp.max(st, axis=0, keepdims=True))
            alpha = jnp.exp2(m_old - m_new)
            pt = jnp.exp2(st - m_new)
            l_scr[g][...] = alpha * l_scr[g][...] + jnp.sum(pt, axis=0, keepdims=True)
            m_scr[g][...] = m_new
            pts.append(pt.astype(BF16))
            alphas.append(alpha)
        for g in range(heads):
            acc_scr[g][...] = alphas[g] * acc_scr[g][...] + _dot(vt_ref[g, :, pl.ds(k0, tq)], pts[g])

    def body(j, carry):
        step(j, False)
        return carry
    lax.fori_loop(0, i, body, 0)
    step(i, True)
    for g in range(heads):
        o_ref[g] = (acc_scr[g][...] / l_scr[g][...]).astype(BF16)


def _fox_attn_prompt(q, k, v, cum, bsz, seq, hd):
    d = q.shape[1]
    nh = d // hd
    heads = _pick(nh, (FOX_HEADS, 2, 1))
    tq = _pick(seq, (FOX_TILE, 128))
    qt = q.reshape(bsz, seq, nh, hd).transpose(0, 2, 3, 1)
    vt = v.reshape(bsz, seq, nh, hd).transpose(0, 2, 3, 1)
    cum_t = cum.transpose(0, 2, 1)
    cq = cum_t[:, :, None, :]
    ck = jnp.broadcast_to(cum_t[..., None], (bsz, nh, seq, LANE))
    ot = pl.pallas_call(
        _fox_prompt_kernel,
        grid=(bsz, nh // heads, seq // tq),
        in_specs=[pl.BlockSpec((None, heads, hd, tq), lambda b, h, i: (b, h, 0, i)),
                  pl.BlockSpec((None, seq, heads * hd), lambda b, h, i: (b, 0, h)),
                  pl.BlockSpec((None, heads, hd, seq), lambda b, h, i: (b, h, 0, 0)),
                  pl.BlockSpec((None, heads, 1, tq), lambda b, h, i: (b, h, 0, i)),
                  pl.BlockSpec((None, heads, seq, LANE), lambda b, h, i: (b, h, 0, 0))],
        out_specs=pl.BlockSpec((None, heads, hd, tq), lambda b, h, i: (b, h, 0, i)),
        out_shape=jax.ShapeDtypeStruct((bsz, nh, hd, seq), BF16),
        scratch_shapes=([pltpu.VMEM((1, tq), F32)] * (2 * heads)
                        + [pltpu.VMEM((hd, tq), F32)] * heads),
        compiler_params=_params(("arbitrary", "arbitrary", "arbitrary")),
        name="fox_attn_prompt",
    )(qt, k.reshape(bsz, seq, d), vt, cq, ck)
    return ot.transpose(0, 3, 1, 2).reshape(bsz * seq, d)


def _router_kernel(x_ref, g_ref, sh_ref, sc_ref, wt_ref, rb_ref, cin_ref,
                   h_ref, ei_ref, ew_ref, rk_ref, cout_ref, cnt):
    @pl.when(pl.program_id(0) == 0)
    def _():
        cnt[...] = cin_ref[...]

    hf = _modulated(x_ref[...], g_ref[...], sh_ref[...], sc_ref[...])
    hh = hf.astype(BF16)
    h_ref[...] = hh
    hl = (hf - hh.astype(F32)).astype(BF16)
    w = wt_ref[...]
    wh = w.astype(BF16)
    wl = (w - wh.astype(F32)).astype(BF16)
    logits = _nt(wh, hh) + (_nt(wh, hl) + _nt(wl, hh))
    scores = jax.nn.sigmoid(logits)
    sel = scores + rb_ref[...]
    ne, tm = sel.shape
    gsz = ne // N_GROUPS
    row_g = lax.broadcasted_iota(jnp.int32, (gsz, tm), 0).astype(F32)
    row_n = lax.broadcasted_iota(jnp.int32, (N_GROUPS, tm), 0).astype(F32)
    row_e = lax.broadcasted_iota(jnp.int32, (ne, tm), 0).astype(F32)

    def first_max(x, rows, big):
        mx = jnp.max(x, axis=0, keepdims=True)
        ix = jnp.min(jnp.where(x == mx, rows, big), axis=0, keepdims=True)
        return mx, ix

    grp = jnp.zeros((N_GROUPS, tm), F32)
    for g in range(N_GROUPS):
        blk = sel[g * gsz:(g + 1) * gsz, :]
        m1, i1 = first_max(blk, row_g, float(gsz))
        m2 = jnp.max(jnp.where(row_g == i1, REMOVED, blk), axis=0, keepdims=True)
        grp = jnp.where(row_n == float(g), m1 + m2, grp)
    gsel = jnp.zeros((N_GROUPS, tm), F32)
    work = grp
    for _ in range(TOPK_GROUPS):
        _, ix = first_max(work, row_n, float(N_GROUPS))
        hit = row_n == ix
        gsel = jnp.where(hit, 1.0, gsel)
        work = jnp.where(hit, REMOVED, work)
    work = jnp.concatenate(
        [jnp.where(gsel[g:g + 1, :] > 0.0, sel[g * gsz:(g + 1) * gsz, :], NEG_INF)
         for g in range(N_GROUPS)], axis=0)
    row_k = lax.broadcasted_iota(jnp.int32, (TOP_K, tm), 0)
    ei = jnp.zeros((TOP_K, tm), F32)
    ew = jnp.zeros((TOP_K, tm), F32)
    hits = []
    chosen = jnp.zeros((ne, tm), F32)
    for r in range(TOP_K):
        _, ix = first_max(work, row_e, float(ne))
        hit = row_e == ix
        hits.append(hit)
        chosen = jnp.where(hit, 1.0, chosen)
        wr = jnp.sum(jnp.where(hit, scores, 0.0), axis=0, keepdims=True)
        ei = jnp.where(row_k == r, ix, ei)
        ew = jnp.where(row_k == r, wr, ew)
        work = jnp.where(hit, REMOVED, work)
    ew = ew / jnp.sum(ew, axis=0, keepdims=True) * ROUTED_SCALE
    ei_ref[...] = ei.astype(jnp.int32)
    ew_ref[...] = ew
    cb = chosen.astype(BF16)
    tr = lax.broadcasted_iota(jnp.int32, (tm, tm), 0)
    tc = lax.broadcasted_iota(jnp.int32, (tm, tm), 1)
    before = _dot(cb, jnp.where(tr < tc, 1.0, 0.0).astype(BF16))
    total = _dot(cb, jnp.ones((tm, LANE), BF16))
    base = cnt[...]
    rank_e = jnp.concatenate([base] * (tm // LANE), axis=1) + before
    rk = jnp.zeros((TOP_K, tm), F32)
    for r in range(TOP_K):
        rr = jnp.sum(jnp.where(hits[r], rank_e, 0.0), axis=0, keepdims=True)
        rk = jnp.where(row_k == r, rr, rk)
    rk_ref[...] = rk.astype(jnp.int32)
    cnt[...] = base + total
    cout_ref[...] = base + total


def _router(x, g3, layer, shift, scale, rows_per_seq, router_wt, router_b3, cnt_in):
    m, d = x.shape
    ne = router_wt.shape[1]
    tm = _pick(m, (256, 128))
    asg = pl.BlockSpec((TOP_K, tm), lambda i: (0, i))
    return pl.pallas_call(
        _router_kernel,
        grid=(m // tm,),
        in_specs=[pl.BlockSpec((tm, d), lambda i: (i, 0)),
                  pl.BlockSpec((None, 1, d), lambda i: (layer, 0, 0)),
                  _mod_spec(shift, tm, d, rows_per_seq, 0, None),
                  _mod_spec(scale, tm, d, rows_per_seq, 0, None),
                  pl.BlockSpec((None, ne, d), lambda i: (layer, 0, 0)),
                  pl.BlockSpec((None, ne, 1), lambda i: (layer, 0, 0)),
                  pl.BlockSpec((ne, LANE), lambda i: (0, 0))],
        out_specs=[pl.BlockSpec((tm, d), lambda i: (i, 0)), asg, asg, asg,
                   pl.BlockSpec((ne, LANE), lambda i: (0, 0))],
        out_shape=[jax.ShapeDtypeStruct((m, d), BF16),
                   jax.ShapeDtypeStruct((TOP_K, m), jnp.int32),
                   jax.ShapeDtypeStruct((TOP_K, m), F32),
                   jax.ShapeDtypeStruct((TOP_K, m), jnp.int32),
                   jax.ShapeDtypeStruct((ne, LANE), F32)],
        scratch_shapes=[pltpu.VMEM((ne, LANE), F32)],
        compiler_params=_params(("arbitrary",)),
        name="router",
    )(x, g3, shift, scale, router_wt, router_b3, cnt_in)


def _experts_kernel(te_ref, nv_ref, x_ref, wg_ref, wu_ref, wd_ref, y_ref, wgb, wub, wdb):
    t = pl.program_id(0)
    e = te_ref[t]
    prev = te_ref[jnp.maximum(t - 1, 0)]

    @pl.when((t == 0) | (e != prev))
    def _():
        wgb[...] = wg_ref[...].astype(BF16)
        wub[...] = wu_ref[...].astype(BF16)
        wdb[...] = wd_ref[...].astype(BF16)

    @pl.when(t < nv_ref[0])
    def _():
        x = x_ref[...]
        a = _dot(x, wgb[...])
        b = _dot(x, wub[...])
        s = (a * jax.nn.sigmoid(a) * b).astype(BF16)
        y_ref[...] = _dot(s, wdb[...]).reshape(y_ref.shape).astype(BF16)

    @pl.when(t >= nv_ref[0])
    def _():
        y_ref[...] = jnp.zeros(y_ref.shape, BF16)


def _experts(xs, tile_expert, n_valid, wg, wu, wd, layer):
    s, d = xs.shape
    f = wg.shape[-1]
    tm = MOE_TILE
    n_tiles = s // tm

    grid_spec = pltpu.PrefetchScalarGridSpec(
        num_scalar_prefetch=2,
        grid=(n_tiles,),
        in_specs=[pl.BlockSpec((tm, d), lambda t, te, nv: (jnp.minimum(t, nv[0] - 1), 0)),
                  pl.BlockSpec((None, None, d, f), lambda t, te, nv: (layer, te[t], 0, 0)),
                  pl.BlockSpec((None, None, d, f), lambda t, te, nv: (layer, te[t], 0, 0)),
                  pl.BlockSpec((None, None, f, d), lambda t, te, nv: (layer, te[t], 0, 0))],
        out_specs=pl.BlockSpec((tm, d // LANE, LANE), lambda t, te, nv: (t, 0, 0)),
        scratch_shapes=[pltpu.VMEM((d, f), BF16), pltpu.VMEM((d, f), BF16), pltpu.VMEM((f, d), BF16)],
    )
    return pl.pallas_call(
        _experts_kernel,
        grid_spec=grid_spec,
        out_shape=jax.ShapeDtypeStruct((s, d // LANE, LANE), BF16),
        compiler_params=_params(("arbitrary",)),
        name="experts",
    )(tile_expert, n_valid, xs, wg, wu, wd)


def _moe_plan(ei, rank, counts, tm):
    kk, t = ei.shape
    ne = counts.shape[0]
    n_asg = kk * t
    n_tiles = n_asg // tm + ne
    i32 = jnp.int32
    ar = jnp.arange(ne, dtype=i32)
    tiles_per = (counts + tm - 1) // tm
    tile_end = jnp.cumsum(tiles_per)
    pstart = (tile_end - tiles_per) * tm
    ustart = jnp.cumsum(counts) - counts
    pos = rank + jnp.sum(jnp.where(ei[..., None] == ar, pstart, 0), axis=-1)
    n_valid = tile_end[-1]
    tile_ids = jnp.arange(n_tiles, dtype=i32)
    te = jnp.sum((tile_end[None, :] <= tile_ids[:, None]).astype(i32), axis=1)
    te = jnp.minimum(te, jnp.sum((tile_end <= n_valid - 1).astype(i32)))
    onehot = te[:, None] == ar[None, :]
    t_pstart, t_count, t_ustart = (jnp.sum(jnp.where(onehot, v, 0), axis=1)
                                   for v in (pstart, counts, ustart))
    within = tile_ids[:, None] * tm + jnp.arange(tm, dtype=i32)[None, :] - t_pstart[:, None]
    valid = (within < t_count[:, None]) & (tile_ids[:, None] < n_valid)
    order = jnp.argsort(ei.T.reshape(-1), stable=True)
    src = jnp.take(order, jnp.clip(t_ustart[:, None] + within, 0, n_asg - 1).reshape(-1)) // kk
    slot_token = jnp.where(valid.reshape(-1), src, 0).astype(i32)
    return slot_token, pos.astype(i32), te.astype(i32), n_valid.reshape(1).astype(i32)


def _combine_kernel(pos_ref, x_ref, ys_ref, w_ref, sa_ref, sd_ref, gate_ref, o_ref, sdb, ybuf, sem):
    @pl.when(pl.program_id(0) == 0)
    def _():
        sdb[...] = sd_ref[...].astype(BF16)

    tm, d = x_ref.shape

    def row_copy(k, r):
        return pltpu.make_async_copy(ys_ref.at[pos_ref[k, r]], ybuf.at[k * tm + r], sem)

    def issue(r, carry):
        for k in range(TOP_K):
            row_copy(k, r).start()
        return carry
    lax.fori_loop(0, tm, issue, 0)

    w = w_ref[...]
    acc = _dot(sa_ref[...], sdb[...])

    def drain(r, carry):
        for k in range(TOP_K):
            row_copy(k, r).wait()
        return carry
    lax.fori_loop(0, tm, drain, 0)

    for k in range(TOP_K):
        yk = ybuf[pl.ds(k * tm, tm)].astype(F32).reshape(tm, d)
        acc = acc + w[:, k:k + 1] * yk
    o_ref[...] = x_ref[...] + gate_ref[...] * acc


def _combine(x, ys, pos, w, sact, sh_w_down, layer, gate, rows_per_seq):
    m, d = x.shape
    f = sact.shape[1]
    tm = _pick(m, (256, 128))
    return pl.pallas_call(
        _combine_kernel,
        grid=(m // tm,),
        in_specs=[pl.BlockSpec((TOP_K, tm), lambda i: (0, i), memory_space=pltpu.SMEM),
                  pl.BlockSpec((tm, d), lambda i: (i, 0)),
                  pl.BlockSpec(memory_space=pl.ANY),
                  pl.BlockSpec((tm, TOP_K), lambda i: (i, 0)),
                  pl.BlockSpec((tm, f), lambda i: (i, 0)),
                  pl.BlockSpec((None, f, d), lambda i: (layer, 0, 0)),
                  _mod_spec(gate, tm, d, rows_per_seq, 0, None)],
        out_specs=pl.BlockSpec((tm, d), lambda i: (i, 0)),
        out_shape=jax.ShapeDtypeStruct((m, d), F32),
        scratch_shapes=[pltpu.VMEM((f, d), BF16),
                        pltpu.VMEM((TOP_K * tm, d // LANE, LANE), BF16),
                        pltpu.SemaphoreType.DMA],
        compiler_params=_params(("arbitrary",)),
        name="combine",
    )(pos, x, ys, w, sact, sh_w_down, gate)


def kernel(x_prompt, x_sample, c_prompt, c_sample, cache_a_k, cache_a_v, cache_b_k, cache_b_v,
           cache_b_logf, ada_w, ada_b, norm1_g, norm2_g, a_w_in, a_q_g, a_k_g, a_rel_table, a_w_out,
           b_w_in, b_f_bias, b_q_g, b_k_g, b_w_out, router_w, router_bias, exp_w_gate, exp_w_up,
           exp_w_down, sh_w_gate, sh_w_up, sh_w_down):
    bsz, seq, d = x_prompt.shape
    dbsz, dseq, _ = x_sample.shape
    depth = ada_w.shape[0]
    hd = a_q_g.shape[-1]
    nh = d // hd
    ne = router_w.shape[-1]
    mp, ms = bsz * seq, dbsz * dseq
    attn_scale = hd ** -0.5 * LOG2E

    xp = x_prompt.reshape(mp, d)
    xs = x_sample.reshape(ms, d)
    c_all = jnp.concatenate([c_prompt, c_sample], axis=0)
    ada_b3 = ada_b.reshape(depth, 1, -1)
    n1g = norm1_g.reshape(depth, 1, d)
    n2g = norm2_g.reshape(depth, 1, d)
    router_wt = jnp.swapaxes(router_w, 1, 2)
    router_b3 = router_bias.reshape(depth, ne, 1)
    sh_gu = jnp.concatenate([sh_w_gate, sh_w_up], axis=-1)

    a_out = {n: [] for n in ("kp", "vp", "ks", "vs")}
    b_out = {n: [] for n in ("kp", "vp", "lp", "ks", "vs", "ls")}

    for i in range(depth):
        mods = _adaln(c_all, ada_w, ada_b3, i)
        mod6 = jnp.split(mods, 6, axis=-1)
        mod_p = [t[:bsz].reshape(bsz, 1, d) for t in mod6]
        mod_s = [jnp.repeat(t[bsz:], dseq, axis=0) for t in mod6]
        hp = _modulate(xp, n1g, i, mod_p[0], mod_p[1], seq)
        hs = _modulate(xs, n1g, i, mod_s[0], mod_s[1], ms)
        j = i // 2
        if i % 2 == 0:
            qg, kg = a_q_g.reshape(-1, 1, hd), a_k_g.reshape(-1, 1, hd)
            proj = {}
            for nm, h_in in (("p", hp), ("s", hs)):
                q, = _mm(h_in, a_w_in, j, 0, d, epi="headnorm", g=qg, scale=attn_scale,
                         out_dtypes=(BF16,), name="a_q")
                kf, kb = _mm(h_in, a_w_in, j, d, d, epi="headnorm", g=kg,
                             out_dtypes=(F32, BF16), head_dim=hd, name="a_k")
                vf, vb = _mm(h_in, a_w_in, j, 2 * d, d, epi="plain",
                             out_dtypes=(F32, BF16), head_dim=hd, name="a_v")
                proj[nm] = (q, kf, kb, vf, vb)
            q, kf, kb, vf, vb = proj["p"]
            bias_p, edge = _chunk_bias_prompt(a_rel_table[j])
            op = _chunk_attn_prompt(q, kb, vb, bias_p, edge, bsz, seq, hd)
            keep = min(N_PREV_CHUNKS * CHUNK, seq)
            a_out["kp"].append(kf.reshape(bsz, seq, nh, hd)[:, seq - keep:])
            a_out["vp"].append(vf.reshape(bsz, seq, nh, hd)[:, seq - keep:])
            q, kf, kb, vf, vb = proj["s"]
            w = cache_a_k.shape[2]
            rel_max = (a_rel_table.shape[-1] - 1) // 2
            rel = np.arange(dseq)[:, None] + w - np.arange(w + dseq)[None, :]
            bias_s = a_rel_table[j][:, np.clip(rel, -rel_max, rel_max) + rel_max].astype(F32) * LOG2E
            os_ = _chunk_attn_sample(q, kb, vb, cache_a_k.reshape(-1, dbsz, w, d),
                                     cache_a_v.reshape(-1, dbsz, w, d), j,
                                     bias_s[:, :, :w], bias_s[:, :, w:], dseq, hd)
            a_out["ks"].append(kf.reshape(dbsz, dseq, nh, hd))
            a_out["vs"].append(vf.reshape(dbsz, dseq, nh, hd))
            mul_p = mul_s = None
            w_out = a_w_out
        else:
            qg, kg = b_q_g.reshape(-1, 1, hd), b_k_g.reshape(-1, 1, hd)
            w_f = jnp.pad(b_w_in[j][:, 3 * d:3 * d + nh], ((0, 0), (0, LANE - nh)))[None]
            f_b = jnp.pad(b_f_bias[j], (0, LANE - nh)).reshape(1, 1, LANE)
            w_g = b_w_in[j][:, 3 * d + nh:][None]
            proj = {}
            for nm, h_in in (("p", hp), ("s", hs)):
                q, = _mm(h_in, b_w_in, j, 0, d, epi="headnorm", g=qg, scale=attn_scale,
                         out_dtypes=(BF16,), name="b_q")
                kf, kb = _mm(h_in, b_w_in, j, d, d, epi="headnorm", g=kg,
                             out_dtypes=(F32, BF16), head_dim=hd, name="b_k")
                vf, vb = _mm(h_in, b_w_in, j, 2 * d, d, epi="plain",
                             out_dtypes=(F32, BF16), head_dim=hd, name="b_v")
                lf, = _mm(h_in, w_f, 0, 0, LANE, epi="logsig", bias=f_b, out_dtypes=(F32,),
                          name="b_logf")
                gs, = _mm(h_in, w_g, 0, 0, d, epi="sigmoid", out_dtypes=(BF16,), name="b_gate")
                proj[nm] = (q, kf, kb, vf, vb, lf[:, :nh], gs)
            q, kf, kb, vf, vb, lf, mul_p = proj["p"]
            lf3 = lf.reshape(bsz, seq, nh)
            op = _fox_attn_prompt(q, kb, vb, jnp.cumsum(lf3, axis=1) * LOG2E, bsz, seq, hd)
            b_out["kp"].append(kf.reshape(bsz, seq, nh, hd))
            b_out["vp"].append(vf.reshape(bsz, seq, nh, hd))
            b_out["lp"].append(lf3)
            q, kf, kb, vf, vb, lf, mul_s = proj["s"]
            past = cache_b_k.shape[2]
            lf3 = lf.reshape(dbsz, dseq, nh)
            lf_all = jnp.concatenate([cache_b_logf[j].astype(F32), lf3], axis=1)
            cum = jnp.cumsum(lf_all, axis=1).transpose(0, 2, 1) * LOG2E
            os_ = _fox_attn_sample(q, kb, vb, cache_b_k.reshape(-1, dbsz, past, d),
                                   cache_b_v.reshape(-1, dbsz, past, d), j,
                                   cum[:, :, past:, None], cum[:, :, None, :past],
                                   cum[:, :, None, past:], dseq, hd)
            b_out["ks"].append(kf.reshape(dbsz, dseq, nh, hd))
            b_out["vs"].append(vf.reshape(dbsz, dseq, nh, hd))
            b_out["ls"].append(lf3)
            w_out = b_w_out
        xp, = _mm(op, w_out, j, 0, d, epi="resid", mul=mul_p, xres=xp, gate=mod_p[2],
                  rows_per_seq=seq, out_dtypes=(F32,), name="out_proj")
        xs, = _mm(os_, w_out, j, 0, d, epi="resid", mul=mul_s, xres=xs, gate=mod_s[2],
                  rows_per_seq=ms, out_dtypes=(F32,), name="out_proj")

        cnt0 = jnp.zeros((ne, LANE), F32)
        h2p, eip, ewp, rkp, cnt1 = _router(xp, n2g, i, mod_p[3], mod_p[4], seq, router_wt, router_b3, cnt0)
        h2s, eis, ews, rks, cnt2 = _router(xs, n2g, i, mod_s[3], mod_s[4], ms, router_wt, router_b3, cnt1)
        h2 = jnp.concatenate([h2p, h2s], axis=0)
        ei = jnp.concatenate([eip, eis], axis=1)
        rank = jnp.concatenate([rkp, rks], axis=1)
        slot_token, pos, tile_expert, n_valid = _moe_plan(ei, rank, cnt2[:, 0].astype(jnp.int32), MOE_TILE)
        xg = h2.at[slot_token].get(mode="promise_in_bounds")
        ys = _experts(xg, tile_expert, n_valid, exp_w_gate, exp_w_up, exp_w_down, i)
        sp, = _mm(h2p, sh_gu, i, 0, sh_gu.shape[-1], epi="swiglu", out_dtypes=(BF16,), name="shared_up")
        ss, = _mm(h2s, sh_gu, i, 0, sh_gu.shape[-1], epi="swiglu", out_dtypes=(BF16,), name="shared_up")
        xp = _combine(xp, ys, pos[:, :mp], ewp.T, sp, sh_w_down, i, mod_p[5], seq)
        xs = _combine(xs, ys, pos[:, mp:], ews.T, ss, sh_w_down, i, mod_s[5], ms)

    st = jnp.stack
    return (xp.reshape(bsz, seq, d), xs.reshape(dbsz, dseq, d),
            st(a_out["kp"]), st(a_out["vp"]), st(a_out["ks"]), st(a_out["vs"]),
            st(b_out["kp"]), st(b_out["vp"]), st(b_out["lp"]),
            st(b_out["ks"]), st(b_out["vs"]), st(b_out["ls"]))
```

```python
import functools

import jax
import jax.numpy as jnp
import numpy as np
from jax import lax
from jax.experimental import pallas as pl
from jax.experimental.pallas import tpu as pltpu

F32 = jnp.float32
BF16 = jnp.bfloat16

CHUNK = 64
N_PREV_CHUNKS = 8
TOP_K = 8
N_GROUPS = 8
TOPK_GROUPS = 4
ROUTED_SCALE = 2.5
NORM_EPS = 1e-6
NEG_INF = -1e30
REMOVED = -3e38
LANE = 128
VMEM_LIMIT = 56 * 1024 * 1024
Q_TILE = 128
BAND_KEYS = (N_PREV_CHUNKS + 2) * CHUNK
FOX_TILE = 256
MOE_TILE = 256
LOG2E = 1.4426950408889634
CHUNK_HEADS = 4
FOX_HEADS = 4


def _pick(n, cands):
    for c in cands:
        if n % c == 0:
            return c
    return n


def _params(sem):
    return pltpu.CompilerParams(dimension_semantics=sem, vmem_limit_bytes=VMEM_LIMIT)


def _nt(a, b):
    return lax.dot_general(a, b, (((1,), (1,)), ((), ())), preferred_element_type=F32)


def _dot(a, b):
    return jnp.dot(a, b, preferred_element_type=F32)


def _mod_spec(mod, tm, tn, rows_per_seq, row_arg, col_arg):
    if mod.ndim == 3:
        def imap(*g):
            col = 0 if col_arg is None else g[col_arg]
            return ((g[row_arg] * tm) // rows_per_seq, 0, col)
        return pl.BlockSpec((None, 1, tn), imap)

    def imap2(*g):
        col = 0 if col_arg is None else g[col_arg]
        return (g[row_arg], col)
    return pl.BlockSpec((tm, tn), imap2)


def _adaln_kernel(c_ref, w_ref, b_ref, o_ref):
    c = c_ref[...]
    a = (c * jax.nn.sigmoid(c)).astype(BF16)
    o_ref[...] = _dot(a, w_ref[...].astype(BF16)) + b_ref[...]


def _adaln(c_all, ada_w, ada_b3, layer):
    r, d = c_all.shape
    n = ada_w.shape[2]
    tn = _pick(n, (1024, 512, 256, 128))
    return pl.pallas_call(
        _adaln_kernel,
        grid=(n // tn,),
        in_specs=[pl.BlockSpec((r, d), lambda j: (0, 0)),
                  pl.BlockSpec((None, d, tn), lambda j: (layer, 0, j)),
                  pl.BlockSpec((None, 1, tn), lambda j: (layer, 0, j))],
        out_specs=pl.BlockSpec((r, tn), lambda j: (0, j)),
        out_shape=jax.ShapeDtypeStruct((r, n), F32),
        compiler_params=_params(("arbitrary",)),
        name="adaln",
    )(c_all, ada_w, ada_b3)


def _modulated(x, g, sh, sc):
    ms = jnp.mean(x * x, axis=-1, keepdims=True)
    y = x * lax.rsqrt(ms + NORM_EPS) * g
    return y * (1.0 + sc) + sh


def _modulate_kernel(x_ref, g_ref, sh_ref, sc_ref, o_ref):
    o_ref[...] = _modulated(x_ref[...], g_ref[...], sh_ref[...], sc_ref[...]).astype(BF16)


def _modulate(x, g3, layer, shift, scale, rows_per_seq):
    m, d = x.shape
    tm = _pick(min(m, rows_per_seq), (512, 256, 128, 64, 32, 16, 8))
    return pl.pallas_call(
        _modulate_kernel,
        grid=(m // tm,),
        in_specs=[pl.BlockSpec((tm, d), lambda i: (i, 0)),
                  pl.BlockSpec((None, 1, d), lambda i: (layer, 0, 0)),
                  _mod_spec(shift, tm, d, rows_per_seq, 0, None),
                  _mod_spec(scale, tm, d, rows_per_seq, 0, None)],
        out_specs=pl.BlockSpec((tm, d), lambda i: (i, 0)),
        out_shape=jax.ShapeDtypeStruct((m, d), BF16),
        compiler_params=_params(("arbitrary",)),
        name="modulate",
    )(x, g3, shift, scale)


def _mm_kernel(*refs, epi, has_mul, n_out, scale):
    it = iter(refs)
    a_ref = next(it)
    m_ref = next(it) if has_mul else None
    w_ref = next(it)
    extra = []
    n_extra = {"headnorm": 1, "plain": 0, "logsig": 1, "sigmoid": 0, "resid": 2, "swiglu": 0}[epi]
    for _ in range(n_extra):
        extra.append(next(it))
    outs = [next(it) for _ in range(n_out)]
    wb_ref = next(it)

    @pl.when(pl.program_id(1) == 0)
    def _():
        wb_ref[...] = w_ref[...].astype(BF16)

    a = a_ref[...]
    if has_mul:
        a = a * m_ref[...]
    acc = _dot(a, wb_ref[...])

    if epi == "headnorm":
        g = extra[0][...]
        hd = g.shape[-1]
        ys = []
        for h in range(acc.shape[1] // hd):
            blk = acc[:, h * hd:(h + 1) * hd]
            ms = jnp.mean(blk * blk, axis=-1, keepdims=True)
            ys.append(blk * lax.rsqrt(ms + NORM_EPS) * g)
        res = jnp.concatenate(ys, axis=1)
    elif epi == "plain":
        res = acc
    elif epi == "logsig":
        z = acc + extra[0][...]
        res = jnp.minimum(z, 0.0) - jnp.log1p(jnp.exp(-jnp.abs(z)))
    elif epi == "sigmoid":
        res = jax.nn.sigmoid(acc)
    elif epi == "resid":
        res = extra[0][...] + extra[1][...] * acc
    elif epi == "swiglu":
        f = acc.shape[1] // 2
        ga = acc[:, :f]
        res = ga * jax.nn.sigmoid(ga) * acc[:, f:]
    for o in outs:
        val = res * scale if (o.dtype == BF16 and scale != 1.0) else res
        if len(o.shape) == 3:
            val = val.reshape(o.shape)
        o[...] = val.astype(o.dtype)


def _mm(a, w, layer, col0, n, *, epi, out_dtypes, mul=None, g=None, bias=None, xres=None,
        gate=None, rows_per_seq=None, scale=1.0, head_dim=None, name="mm"):
    m, k = a.shape
    if epi == "swiglu":
        tn = n
    else:
        tn = _pick(n, (1024, 512, 256, 128))
    assert col0 % tn == 0
    cb = col0 // tn
    tm = _pick(m, (512, 256, 128))
    if rows_per_seq is not None:
        assert rows_per_seq % tm == 0 or (gate is not None and gate.ndim == 2)
    n_out_cols = n // 2 if epi == "swiglu" else n
    tno = tn // 2 if epi == "swiglu" else tn

    args = [a]
    specs = [pl.BlockSpec((tm, k), lambda j, i: (i, 0))]
    if mul is not None:
        args.append(mul)
        specs.append(pl.BlockSpec((tm, k), lambda j, i: (i, 0)))
    args.append(w)
    specs.append(pl.BlockSpec((None, k, tn), lambda j, i: (layer, 0, cb + j)))
    if epi == "headnorm":
        args.append(g)
        specs.append(pl.BlockSpec((None, 1, g.shape[-1]), lambda j, i: (layer, 0, 0)))
    elif epi == "logsig":
        args.append(bias)
        specs.append(pl.BlockSpec((None, 1, tn), lambda j, i: (layer, 0, j)))
    elif epi == "resid":
        args.append(xres)
        specs.append(pl.BlockSpec((tm, tn), lambda j, i: (i, j)))
        args.append(gate)
        specs.append(_mod_spec(gate, tm, tn, rows_per_seq, 1, 0))
    out_shape, out_specs = [], []
    for dt in out_dtypes:
        if head_dim is not None and dt == F32:
            out_shape.append(jax.ShapeDtypeStruct((m, n_out_cols // head_dim, head_dim), dt))
            out_specs.append(pl.BlockSpec((tm, tno // head_dim, head_dim), lambda j, i: (i, j, 0)))
        else:
            out_shape.append(jax.ShapeDtypeStruct((m, n_out_cols), dt))
            out_specs.append(pl.BlockSpec((tm, tno), lambda j, i: (i, j)))
    res = pl.pallas_call(
        functools.partial(_mm_kernel, epi=epi, has_mul=mul is not None, n_out=len(out_dtypes),
                          scale=scale),
        grid=(n // tn, m // tm),
        in_specs=specs,
        out_specs=out_specs,
        out_shape=out_shape,
        scratch_shapes=[pltpu.VMEM((k, tn), BF16)],
        compiler_params=_params(("arbitrary", "arbitrary")),
        name=name,
    )(*args)
    return res


def _chunk_prompt_kernel(q_ref, k_ref, v_ref, b_ref, o_ref, *, n_tiles, edge, hd):
    heads = q_ref.shape[1] // hd

    def body(m, carry):
        q0 = pl.multiple_of(m * Q_TILE, Q_TILE)
        k0 = pl.multiple_of(jnp.maximum(m - edge, 0) * Q_TILE, Q_TILE)
        bi = jnp.minimum(m, edge)
        sls = [slice(g * hd, (g + 1) * hd) for g in range(heads)]
        ss = [_nt(q_ref[pl.ds(q0, Q_TILE), sl], k_ref[pl.ds(k0, BAND_KEYS), sl]) for sl in sls]
        ps, ls = [], []
        for g in range(heads):
            s = ss[g] + b_ref[g, bi]
            mx = jnp.max(s, axis=-1, keepdims=True)
            p = jnp.exp2(s - mx)
            ls.append(jnp.sum(p, axis=-1, keepdims=True))
            ps.append(p.astype(BF16))
        for g in range(heads):
            o = _dot(ps[g], v_ref[pl.ds(k0, BAND_KEYS), sls[g]]) / ls[g]
            o_ref[pl.ds(q0, Q_TILE), sls[g]] = o.astype(BF16)
        return carry
    lax.fori_loop(0, n_tiles, body, 0)


def _chunk_bias_prompt(table):
    nh = table.shape[0]
    rel_max = (table.shape[1] - 1) // 2
    edge = (BAND_KEYS - Q_TILE) // Q_TILE
    mm = np.arange(edge + 1)[:, None, None]
    qpos = mm * Q_TILE + np.arange(Q_TILE)[None, :, None]
    kpos = np.maximum(mm - edge, 0) * Q_TILE + np.arange(BAND_KEYS)[None, None, :]
    qc, kc = qpos // CHUNK, kpos // CHUNK
    vis = (kc <= qc) & (kc >= qc - N_PREV_CHUNKS)
    period = Q_TILE + BAND_KEYS - 1
    c_m = np.minimum(np.arange(edge + 1), edge) * Q_TILE
    n = np.arange(period)
    n = np.where(n < BAND_KEYS, n, n - period)[None, :]
    idx = np.clip(c_m[:, None] - n, -rel_max, rel_max) + rel_max
    w = table[:, idx].astype(F32) * LOG2E
    skew = jnp.tile(w, (1, 1, Q_TILE))[..., :Q_TILE * (period - 1)]
    b = skew.reshape(nh, edge + 1, Q_TILE, period - 1)[..., :BAND_KEYS]
    return jnp.where(vis[None], b, NEG_INF), edge


def _chunk_attn_prompt(q, k, v, bias, edge, bsz, seq, hd):
    d = q.shape[1]
    nh = d // hd
    heads = _pick(nh, (CHUNK_HEADS, 2, 1))
    q3, k3, v3 = (t.reshape(bsz, seq, d) for t in (q, k, v))
    spec = pl.BlockSpec((None, seq, heads * hd), lambda b, h: (b, 0, h))
    o = pl.pallas_call(
        functools.partial(_chunk_prompt_kernel, n_tiles=seq // Q_TILE, edge=edge, hd=hd),
        grid=(bsz, nh // heads),
        in_specs=[spec, spec, spec,
                  pl.BlockSpec((heads, edge + 1, Q_TILE, BAND_KEYS), lambda b, h: (h, 0, 0, 0))],
        out_specs=spec,
        out_shape=jax.ShapeDtypeStruct((bsz, seq, d), BF16),
        compiler_params=_params(("arbitrary", "arbitrary")),
        name="chunk_attn_prompt",
    )(q3, k3, v3, bias)
    return o.reshape(bsz * seq, d)


def _sample_attn_kernel(*refs, hd, fox):
    if fox:
        q_ref, kn_ref, vn_ref, kc_ref, vc_ref, cq_ref, ckc_ref, ckn_ref, o_ref = refs
    else:
        q_ref, kn_ref, vn_ref, kc_ref, vc_ref, bc_ref, bn_ref, o_ref = refs
    t = q_ref.shape[0]
    nh = q_ref.shape[1] // hd
    if fox:
        row = lax.broadcasted_iota(jnp.int32, (t, t), 0)
        col = lax.broadcasted_iota(jnp.int32, (t, t), 1)
        causal = col <= row
    for h in range(nh):
        sl = slice(h * hd, (h + 1) * hd)
        q = q_ref[:, sl]
        kc = kc_ref[:, sl].astype(BF16)
        vc = vc_ref[:, sl].astype(BF16)
        s1 = _nt(q, kc)
        s2 = _nt(q, kn_ref[:, sl])
        if fox:
            cq = cq_ref[h]
            s1 = s1 + cq - ckc_ref[h]
            s2 = jnp.where(causal, s2 + cq - ckn_ref[h], NEG_INF)
        else:
            s1 = s1 + bc_ref[h]
            s2 = s2 + bn_ref[h]
        mx = jnp.maximum(jnp.max(s1, axis=-1, keepdims=True), jnp.max(s2, axis=-1, keepdims=True))
        p1 = jnp.exp2(s1 - mx)
        p2 = jnp.exp2(s2 - mx)
        l = jnp.sum(p1, axis=-1, keepdims=True) + jnp.sum(p2, axis=-1, keepdims=True)
        o = (_dot(p1.astype(BF16), vc) + _dot(p2.astype(BF16), vn_ref[:, sl])) / l
        o_ref[:, sl] = o.astype(BF16)


def _chunk_attn_sample(q, kn, vn, cache_k, cache_v, layer, bias_c, bias_n, t, hd):
    _, bsz, w, d = cache_k.shape
    nh = d // hd
    hg = _pick(nh, (8, 4, 2, 1)) * hd
    ng = d // hg
    tok = pl.BlockSpec((t, hg), lambda b, g: (b, g))
    cache = pl.BlockSpec((None, None, w, hg), lambda b, g: (layer, b, 0, g))
    return pl.pallas_call(
        functools.partial(_sample_attn_kernel, hd=hd, fox=False),
        grid=(bsz, ng),
        in_specs=[tok, tok, tok, cache, cache,
                  pl.BlockSpec((hg // hd, t, w), lambda b, g: (g, 0, 0)),
                  pl.BlockSpec((hg // hd, t, t), lambda b, g: (g, 0, 0))],
        out_specs=tok,
        out_shape=jax.ShapeDtypeStruct((bsz * t, d), BF16),
        compiler_params=_params(("arbitrary", "arbitrary")),
        name="chunk_attn_sample",
    )(q, kn, vn, cache_k, cache_v, bias_c, bias_n)


def _fox_attn_sample(q, kn, vn, cache_k, cache_v, layer, cq, ckc, ckn, t, hd):
    _, bsz, w, d = cache_k.shape
    nh = d // hd
    hg = _pick(nh, (4, 2, 1)) * hd
    ng = d // hg
    gh = hg // hd
    tok = pl.BlockSpec((t, hg), lambda b, g: (b, g))
    cache = pl.BlockSpec((None, None, w, hg), lambda b, g: (layer, b, 0, g))
    return pl.pallas_call(
        functools.partial(_sample_attn_kernel, hd=hd, fox=True),
        grid=(bsz, ng),
        in_specs=[tok, tok, tok, cache, cache,
                  pl.BlockSpec((None, gh, t, 1), lambda b, g: (b, g, 0, 0)),
                  pl.BlockSpec((None, gh, 1, w), lambda b, g: (b, g, 0, 0)),
                  pl.BlockSpec((None, gh, 1, t), lambda b, g: (b, g, 0, 0))],
        out_specs=tok,
        out_shape=jax.ShapeDtypeStruct((bsz * t, d), BF16),
        compiler_params=_params(("arbitrary", "arbitrary")),
        name="fox_attn_sample",
    )(q, kn, vn, cache_k, cache_v, cq, ckc, ckn)


def _fox_prompt_kernel(q_ref, k_ref, v_ref, cq_ref, ck_ref, o_ref, *scr):
    i = pl.program_id(2)
    tq = q_ref.shape[0]
    heads = cq_ref.shape[0]
    hd = q_ref.shape[1] // heads
    m_scr, l_scr, acc_scr = scr[:heads], scr[heads:2 * heads], scr[2 * heads:3 * heads]
    qt_scr = scr[3 * heads:]
    row = lax.broadcasted_iota(jnp.int32, (tq, tq), 0)
    col = lax.broadcasted_iota(jnp.int32, (tq, tq), 1)
    for g in range(heads):
        m_scr[g][...] = jnp.full((1, tq), NEG_INF, F32)
        l_scr[g][...] = jnp.zeros((1, tq), F32)
        acc_scr[g][...] = jnp.zeros(acc_scr[g].shape, F32)
        qt_scr[g][...] = q_ref[:, g * hd:(g + 1) * hd].astype(F32).T.astype(BF16)

    def step(j, diag):
        k0 = pl.multiple_of(j * tq, tq)
        sts = [_dot(k_ref[pl.ds(k0, tq), g * hd:(g + 1) * hd], qt_scr[g][...]) for g in range(heads)]
        pts, alphas = [], []
        for g in range(heads):
            ck = ck_ref[g, pl.ds(k0, tq), :]
            st = sts[g] + cq_ref[g] - jnp.concatenate([ck] * (tq // LANE), axis=1)
            if diag:
                st = jnp.where(row <= col, st, NEG_INF)
            m_old = m_scr[g][...]
            m_new = jnp.maximum(m_old, jnp.max(st, axis=0, keepdims=True))
            alpha = jnp.exp2(m_old - m_new)
            pt = jnp.exp2(st - m_new)
            l_scr[g][...] = alpha * l_scr[g][...] + jnp.sum(pt, axis=0, keepdims=True)
            m_scr[g][...] = m_new
            pts.append(pt.astype(BF16))
            alphas.append(alpha)
        for g in range(heads):
            vb = v_ref[pl.ds(k0, tq), g * hd:(g + 1) * hd]
            pv = lax.dot_general(vb, pts[g], (((0,), (0,)), ((), ())), preferred_element_type=F32)
            acc_scr[g][...] = alphas[g] * acc_scr[g][...] + pv

    def body(j, carry):
        step(j, False)
        return carry
    lax.fori_loop(0, i, body, 0)
    step(i, True)
    for g in range(heads):
        o_ref[:, g * hd:(g + 1) * hd] = (acc_scr[g][...] / l_scr[g][...]).T.astype(BF16)


def _cumsum_rows(x):
    b, s, h = x.shape
    pad = -s % LANE
    xb = jnp.pad(x, ((0, 0), (0, pad), (0, 0))).reshape(b, (s + pad) // LANE, LANE, h)
    tri = jnp.tril(jnp.ones((LANE, LANE), F32))
    within = jnp.einsum("ij,bnjh->bnih", tri, xb, precision=lax.Precision.HIGHEST)
    tot = within[:, :, -1, :]
    offs = jnp.cumsum(tot, axis=1) - tot
    return (within + offs[:, :, None, :]).reshape(b, s + pad, h)[:, :s]


def _fox_attn_prompt(q, k, v, cum, bsz, seq, hd):
    d = q.shape[1]
    nh = d // hd
    heads = _pick(nh, (FOX_HEADS, 2, 1))
    tq = _pick(seq, (FOX_TILE, 128))
    cum_t = cum.transpose(0, 2, 1)
    cq = cum_t[:, :, None, :]
    ck = jnp.broadcast_to(cum_t[..., None], (bsz, nh, seq, LANE))
    q3, k3, v3 = (t.reshape(bsz, seq, d) for t in (q, k, v))
    qspec = pl.BlockSpec((None, tq, heads * hd), lambda b, h, i: (b, i, h))
    kspec = pl.BlockSpec((None, seq, heads * hd), lambda b, h, i: (b, 0, h))
    o = pl.pallas_call(
        _fox_prompt_kernel,
        grid=(bsz, nh // heads, seq // tq),
        in_specs=[qspec, kspec, kspec,
                  pl.BlockSpec((None, heads, 1, tq), lambda b, h, i: (b, h, 0, i)),
                  pl.BlockSpec((None, heads, seq, LANE), lambda b, h, i: (b, h, 0, 0))],
        out_specs=qspec,
        out_shape=jax.ShapeDtypeStruct((bsz, seq, d), BF16),
        scratch_shapes=([pltpu.VMEM((1, tq), F32)] * (2 * heads)
                        + [pltpu.VMEM((hd, tq), F32)] * heads
                        + [pltpu.VMEM((hd, tq), BF16)] * heads),
        compiler_params=_params(("arbitrary", "arbitrary", "arbitrary")),
        name="fox_attn_prompt",
    )(q3, k3, v3, cq, ck)
    return o.reshape(bsz * seq, d)


def _router_kernel(x_ref, g_ref, sh_ref, sc_ref, wt_ref, rb_ref, cin_ref,
                   h_ref, h3_ref, ei_ref, ew_ref, rk_ref, cout_ref, cnt):
    @pl.when(pl.program_id(0) == 0)
    def _():
        cnt[...] = cin_ref[...]

    hf = _modulated(x_ref[...], g_ref[...], sh_ref[...], sc_ref[...])
    hh = hf.astype(BF16)
    h_ref[...] = hh
    h3_ref[...] = hf.reshape(h3_ref.shape).astype(BF16)
    hl = (hf - hh.astype(F32)).astype(BF16)
    w = wt_ref[...]
    wh = w.astype(BF16)
    wl = (w - wh.astype(F32)).astype(BF16)
    logits = _nt(wh, hh) + (_nt(wh, hl) + _nt(wl, hh))
    scores = jax.nn.sigmoid(logits)
    sel = scores + rb_ref[...]
    ne, tm = sel.shape
    gsz = ne // N_GROUPS
    row_g = lax.broadcasted_iota(jnp.int32, (gsz, tm), 0).astype(F32)
    row_n = lax.broadcasted_iota(jnp.int32, (N_GROUPS, tm), 0).astype(F32)
    row_e = lax.broadcasted_iota(jnp.int32, (ne, tm), 0).astype(F32)

    def first_max(x, rows, big):
        mx = jnp.max(x, axis=0, keepdims=True)
        ix = jnp.min(jnp.where(x == mx, rows, big), axis=0, keepdims=True)
        return mx, ix

    grp = jnp.zeros((N_GROUPS, tm), F32)
    for g in range(N_GROUPS):
        blk = sel[g * gsz:(g + 1) * gsz, :]
        m1, i1 = first_max(blk, row_g, float(gsz))
        m2 = jnp.max(jnp.where(row_g == i1, REMOVED, blk), axis=0, keepdims=True)
        grp = jnp.where(row_n == float(g), m1 + m2, grp)
    gsel = jnp.zeros((N_GROUPS, tm), F32)
    work = grp
    for _ in range(TOPK_GROUPS):
        _, ix = first_max(work, row_n, float(N_GROUPS))
        hit = row_n == ix
        gsel = jnp.where(hit, 1.0, gsel)
        work = jnp.where(hit, REMOVED, work)
    work = jnp.concatenate(
        [jnp.where(gsel[g:g + 1, :] > 0.0, sel[g * gsz:(g + 1) * gsz, :], NEG_INF)
         for g in range(N_GROUPS)], axis=0)
    row_k = lax.broadcasted_iota(jnp.int32, (TOP_K, tm), 0)
    ei = jnp.zeros((TOP_K, tm), F32)
    ew = jnp.zeros((TOP_K, tm), F32)
    hits = []
    chosen = jnp.zeros((ne, tm), F32)
    for r in range(TOP_K):
        _, ix = first_max(work, row_e, float(ne))
        hit = row_e == ix
        hits.append(hit)
        chosen = jnp.where(hit, 1.0, chosen)
        wr = jnp.sum(jnp.where(hit, scores, 0.0), axis=0, keepdims=True)
        ei = jnp.where(row_k == r, ix, ei)
        ew = jnp.where(row_k == r, wr, ew)
        work = jnp.where(hit, REMOVED, work)
    ew = ew / jnp.sum(ew, axis=0, keepdims=True) * ROUTED_SCALE
    ei_ref[...] = ei.astype(jnp.int32)
    ew_ref[...] = ew
    cb = chosen.astype(BF16)
    tr = lax.broadcasted_iota(jnp.int32, (tm, tm), 0)
    tc = lax.broadcasted_iota(jnp.int32, (tm, tm), 1)
    before = _dot(cb, jnp.where(tr < tc, 1.0, 0.0).astype(BF16))
    total = _dot(cb, jnp.ones((tm, LANE), BF16))
    base = cnt[...]
    rank_e = jnp.concatenate([base] * (tm // LANE), axis=1) + before
    rk = jnp.zeros((TOP_K, tm), F32)
    for r in range(TOP_K):
        rr = jnp.sum(jnp.where(hits[r], rank_e, 0.0), axis=0, keepdims=True)
        rk = jnp.where(row_k == r, rr, rk)
    rk_ref[...] = rk.astype(jnp.int32)
    cnt[...] = base + total
    cout_ref[...] = base + total


def _router(x, g3, layer, shift, scale, rows_per_seq, router_wt, router_b3, cnt_in):
    m, d = x.shape
    ne = router_wt.shape[1]
    tm = _pick(m, (256, 128))
    asg = pl.BlockSpec((TOP_K, tm), lambda i: (0, i))
    return pl.pallas_call(
        _router_kernel,
        grid=(m // tm,),
        in_specs=[pl.BlockSpec((tm, d), lambda i: (i, 0)),
                  pl.BlockSpec((None, 1, d), lambda i: (layer, 0, 0)),
                  _mod_spec(shift, tm, d, rows_per_seq, 0, None),
                  _mod_spec(scale, tm, d, rows_per_seq, 0, None),
                  pl.BlockSpec((None, ne, d), lambda i: (layer, 0, 0)),
                  pl.BlockSpec((None, ne, 1), lambda i: (layer, 0, 0)),
                  pl.BlockSpec((ne, LANE), lambda i: (0, 0))],
        out_specs=[pl.BlockSpec((tm, d), lambda i: (i, 0)),
                   pl.BlockSpec((tm, d // LANE, LANE), lambda i: (i, 0, 0)), asg, asg, asg,
                   pl.BlockSpec((ne, LANE), lambda i: (0, 0))],
        out_shape=[jax.ShapeDtypeStruct((m, d), BF16),
                   jax.ShapeDtypeStruct((m, d // LANE, LANE), BF16),
                   jax.ShapeDtypeStruct((TOP_K, m), jnp.int32),
                   jax.ShapeDtypeStruct((TOP_K, m), F32),
                   jax.ShapeDtypeStruct((TOP_K, m), jnp.int32),
                   jax.ShapeDtypeStruct((ne, LANE), F32)],
        scratch_shapes=[pltpu.VMEM((ne, LANE), F32)],
        compiler_params=_params(("arbitrary",)),
        name="router",
    )(x, g3, shift, scale, router_wt, router_b3, cnt_in)


def _experts_kernel(te_ref, nv_ref, tok_ref, nxt_ref, h_ref, wg_ref, wu_ref, wd_ref, y_ref,
                    wgb, wub, wdb, xbuf, sem):
    t = pl.program_id(0)
    nv = nv_ref[0]
    tm = xbuf.shape[1]
    d = wg_ref.shape[0]
    slot = lax.rem(t, 2)
    e = te_ref[t]
    prev = te_ref[jnp.maximum(t - 1, 0)]

    def row_copy(idx_ref, r, s):
        return pltpu.make_async_copy(h_ref.at[idx_ref[0, r]], xbuf.at[s, r], sem.at[s])

    @pl.when(t == 0)
    def _():
        for r in range(tm):
            row_copy(tok_ref, r, 0).start()

    @pl.when((t == 0) | (e != prev))
    def _():
        wgb[...] = wg_ref[...].astype(BF16)
        wub[...] = wu_ref[...].astype(BF16)
        wdb[...] = wd_ref[...].astype(BF16)

    @pl.when(t < nv)
    def _():
        for r in range(tm):
            row_copy(tok_ref, r, slot).wait()
        for r in range(tm):
            row_copy(nxt_ref, r, 1 - slot).start()
        x = xbuf[slot].astype(F32).reshape(tm, d).astype(BF16)
        a = _dot(x, wgb[...])
        b = _dot(x, wub[...])
        s = (a * jax.nn.sigmoid(a) * b).astype(BF16)
        y_ref[...] = _dot(s, wdb[...]).reshape(y_ref.shape).astype(BF16)

    @pl.when(t == nv - 1)
    def _():
        for r in range(tm):
            row_copy(nxt_ref, r, 1 - slot).wait()

    @pl.when(t >= nv)
    def _():
        y_ref[...] = jnp.zeros(y_ref.shape, BF16)


def _experts(h3, slot_token, tile_expert, n_valid, wg, wu, wd, layer):
    _, nl, lane = h3.shape
    d = nl * lane
    f = wg.shape[-1]
    tm = MOE_TILE
    n_tiles = slot_token.shape[0] // tm
    tok3 = slot_token.reshape(n_tiles, 1, tm)

    def tok_spec(shift):
        return pl.BlockSpec((None, 1, tm), lambda t, te, nv: (jnp.minimum(t + shift, nv[0] - 1), 0, 0),
                            memory_space=pltpu.SMEM)
    grid_spec = pltpu.PrefetchScalarGridSpec(
        num_scalar_prefetch=2,
        grid=(n_tiles,),
        in_specs=[tok_spec(0), tok_spec(1),
                  pl.BlockSpec(memory_space=pl.ANY),
                  pl.BlockSpec((None, None, d, f), lambda t, te, nv: (layer, te[t], 0, 0)),
                  pl.BlockSpec((None, None, d, f), lambda t, te, nv: (layer, te[t], 0, 0)),
                  pl.BlockSpec((None, None, f, d), lambda t, te, nv: (layer, te[t], 0, 0))],
        out_specs=pl.BlockSpec((tm, nl, lane), lambda t, te, nv: (t, 0, 0)),
        scratch_shapes=[pltpu.VMEM((d, f), BF16), pltpu.VMEM((d, f), BF16), pltpu.VMEM((f, d), BF16),
                        pltpu.VMEM((2, tm, nl, lane), BF16), pltpu.SemaphoreType.DMA((2,))],
    )
    return pl.pallas_call(
        _experts_kernel,
        grid_spec=grid_spec,
        out_shape=jax.ShapeDtypeStruct((n_tiles * tm, nl, lane), BF16),
        compiler_params=_params(("arbitrary",)),
        name="experts",
    )(tile_expert, n_valid, tok3, tok3, h3, wg, wu, wd)


def _moe_plan(ei, rank, counts, tm):
    kk, t = ei.shape
    ne = counts.shape[0]
    n_asg = kk * t
    n_tiles = n_asg // tm + ne
    i32 = jnp.int32
    ar = jnp.arange(ne, dtype=i32)
    tiles_per = (counts + tm - 1) // tm
    tile_end = jnp.cumsum(tiles_per)
    pstart = (tile_end - tiles_per) * tm
    ustart = jnp.cumsum(counts) - counts
    pos = rank + jnp.sum(jnp.where(ei[..., None] == ar, pstart, 0), axis=-1)
    n_valid = tile_end[-1]
    tile_ids = jnp.arange(n_tiles, dtype=i32)
    te = jnp.sum((tile_end[None, :] <= tile_ids[:, None]).astype(i32), axis=1)
    te = jnp.minimum(te, jnp.sum((tile_end <= n_valid - 1).astype(i32)))
    onehot = te[:, None] == ar[None, :]
    t_pstart, t_count, t_ustart = (jnp.sum(jnp.where(onehot, v, 0), axis=1)
                                   for v in (pstart, counts, ustart))
    within = tile_ids[:, None] * tm + jnp.arange(tm, dtype=i32)[None, :] - t_pstart[:, None]
    valid = (within < t_count[:, None]) & (tile_ids[:, None] < n_valid)
    order = jnp.argsort(ei.T.reshape(-1), stable=True)
    src = jnp.take(order, jnp.clip(t_ustart[:, None] + within, 0, n_asg - 1).reshape(-1)) // kk
    slot_token = jnp.where(valid.reshape(-1), src, 0).astype(i32)
    return slot_token, pos.astype(i32), te.astype(i32), n_valid.reshape(1).astype(i32)


def _combine_kernel(pos_ref, x_ref, ys_ref, w_ref, sa_ref, sd_ref, gate_ref, o_ref, sdb, ybuf, sem):
    @pl.when(pl.program_id(0) == 0)
    def _():
        sdb[...] = sd_ref[...].astype(BF16)

    tm, d = x_ref.shape

    def row_copy(k, r):
        return pltpu.make_async_copy(ys_ref.at[pos_ref[k, r]], ybuf.at[k * tm + r], sem)

    def issue(r, carry):
        for k in range(TOP_K):
            row_copy(k, r).start()
        return carry
    lax.fori_loop(0, tm, issue, 0)

    w = w_ref[...]
    acc = _dot(sa_ref[...], sdb[...])

    def drain(r, carry):
        for k in range(TOP_K):
            row_copy(k, r).wait()
        return carry
    lax.fori_loop(0, tm, drain, 0)

    for k in range(TOP_K):
        yk = ybuf[pl.ds(k * tm, tm)].astype(F32).reshape(tm, d)
        acc = acc + w[:, k:k + 1] * yk
    o_ref[...] = x_ref[...] + gate_ref[...] * acc


def _combine(x, ys, pos, w, sact, sh_w_down, layer, gate, rows_per_seq):
    m, d = x.shape
    f = sact.shape[1]
    tm = _pick(m, (256, 128))
    return pl.pallas_call(
        _combine_kernel,
        grid=(m // tm,),
        in_specs=[pl.BlockSpec((TOP_K, tm), lambda i: (0, i), memory_space=pltpu.SMEM),
                  pl.BlockSpec((tm, d), lambda i: (i, 0)),
                  pl.BlockSpec(memory_space=pl.ANY),
                  pl.BlockSpec((tm, TOP_K), lambda i: (i, 0)),
                  pl.BlockSpec((tm, f), lambda i: (i, 0)),
                  pl.BlockSpec((None, f, d), lambda i: (layer, 0, 0)),
                  _mod_spec(gate, tm, d, rows_per_seq, 0, None)],
        out_specs=pl.BlockSpec((tm, d), lambda i: (i, 0)),
        out_shape=jax.ShapeDtypeStruct((m, d), F32),
        scratch_shapes=[pltpu.VMEM((f, d), BF16),
                        pltpu.VMEM((TOP_K * tm, d // LANE, LANE), BF16),
                        pltpu.SemaphoreType.DMA],
        compiler_params=_params(("arbitrary",)),
        name="combine",
    )(pos, x, ys, w, sact, sh_w_down, gate)


def kernel(x_prompt, x_sample, c_prompt, c_sample, cache_a_k, cache_a_v, cache_b_k, cache_b_v,
           cache_b_logf, ada_w, ada_b, norm1_g, norm2_g, a_w_in, a_q_g, a_k_g, a_rel_table, a_w_out,
           b_w_in, b_f_bias, b_q_g, b_k_g, b_w_out, router_w, router_bias, exp_w_gate, exp_w_up,
           exp_w_down, sh_w_gate, sh_w_up, sh_w_down):
    bsz, seq, d = x_prompt.shape
    dbsz, dseq, _ = x_sample.shape
    depth = ada_w.shape[0]
    hd = a_q_g.shape[-1]
    nh = d // hd
    ne = router_w.shape[-1]
    mp, ms = bsz * seq, dbsz * dseq
    attn_scale = hd ** -0.5 * LOG2E

    xp = x_prompt.reshape(mp, d)
    xs = x_sample.reshape(ms, d)
    c_all = jnp.concatenate([c_prompt, c_sample], axis=0)
    ada_b3 = ada_b.reshape(depth, 1, -1)
    n1g = norm1_g.reshape(depth, 1, d)
    n2g = norm2_g.reshape(depth, 1, d)
    router_wt = jnp.swapaxes(router_w, 1, 2)
    router_b3 = router_bias.reshape(depth, ne, 1)
    sh_gu = jnp.concatenate([sh_w_gate, sh_w_up], axis=-1)

    a_out = {n: [] for n in ("kp", "vp", "ks", "vs")}
    b_out = {n: [] for n in ("kp", "vp", "lp", "ks", "vs", "ls")}

    for i in range(depth):
        mods = _adaln(c_all, ada_w, ada_b3, i)
        mod6 = jnp.split(mods, 6, axis=-1)
        mod_p = [t[:bsz].reshape(bsz, 1, d) for t in mod6]
        mod_s = [jnp.repeat(t[bsz:], dseq, axis=0) for t in mod6]
        hp = _modulate(xp, n1g, i, mod_p[0], mod_p[1], seq)
        hs = _modulate(xs, n1g, i, mod_s[0], mod_s[1], ms)
        j = i // 2
        if i % 2 == 0:
            qg, kg = a_q_g.reshape(-1, 1, hd), a_k_g.reshape(-1, 1, hd)
            proj = {}
            for nm, h_in in (("p", hp), ("s", hs)):
                q, = _mm(h_in, a_w_in, j, 0, d, epi="headnorm", g=qg, scale=attn_scale,
                         out_dtypes=(BF16,), name="a_q")
                kf, kb = _mm(h_in, a_w_in, j, d, d, epi="headnorm", g=kg,
                             out_dtypes=(F32, BF16), head_dim=hd, name="a_k")
                vf, vb = _mm(h_in, a_w_in, j, 2 * d, d, epi="plain",
                             out_dtypes=(F32, BF16), head_dim=hd, name="a_v")
                proj[nm] = (q, kf, kb, vf, vb)
            q, kf, kb, vf, vb = proj["p"]
            bias_p, edge = _chunk_bias_prompt(a_rel_table[j])
            op = _chunk_attn_prompt(q, kb, vb, bias_p, edge, bsz, seq, hd)
            keep = min(N_PREV_CHUNKS * CHUNK, seq)
            a_out["kp"].append(kf.reshape(bsz, seq, nh, hd)[:, seq - keep:])
            a_out["vp"].append(vf.reshape(bsz, seq, nh, hd)[:, seq - keep:])
            q, kf, kb, vf, vb = proj["s"]
            w = cache_a_k.shape[2]
            rel_max = (a_rel_table.shape[-1] - 1) // 2
            rel = np.arange(dseq)[:, None] + w - np.arange(w + dseq)[None, :]
            bias_s = a_rel_table[j][:, np.clip(rel, -rel_max, rel_max) + rel_max].astype(F32) * LOG2E
            os_ = _chunk_attn_sample(q, kb, vb, cache_a_k.reshape(-1, dbsz, w, d),
                                     cache_a_v.reshape(-1, dbsz, w, d), j,
                                     bias_s[:, :, :w], bias_s[:, :, w:], dseq, hd)
            a_out["ks"].append(kf.reshape(dbsz, dseq, nh, hd))
            a_out["vs"].append(vf.reshape(dbsz, dseq, nh, hd))
            mul_p = mul_s = None
            w_out = a_w_out
        else:
            qg, kg = b_q_g.reshape(-1, 1, hd), b_k_g.reshape(-1, 1, hd)
            w_f = jnp.pad(b_w_in[j][:, 3 * d:3 * d + nh], ((0, 0), (0, LANE - nh)))[None]
            f_b = jnp.pad(b_f_bias[j], (0, LANE - nh)).reshape(1, 1, LANE)
            w_g = b_w_in[j][:, 3 * d + nh:][None]
            proj = {}
            for nm, h_in in (("p", hp), ("s", hs)):
                q, = _mm(h_in, b_w_in, j, 0, d, epi="headnorm", g=qg, scale=attn_scale,
                         out_dtypes=(BF16,), name="b_q")
                kf, kb = _mm(h_in, b_w_in, j, d, d, epi="headnorm", g=kg,
                             out_dtypes=(F32, BF16), head_dim=hd, name="b_k")
                vf, vb = _mm(h_in, b_w_in, j, 2 * d, d, epi="plain",
                             out_dtypes=(F32, BF16), head_dim=hd, name="b_v")
                lf, = _mm(h_in, w_f, 0, 0, LANE, epi="logsig", bias=f_b, out_dtypes=(F32,),
                          name="b_logf")
                gs, = _mm(h_in, w_g, 0, 0, d, epi="sigmoid", out_dtypes=(BF16,), name="b_gate")
                proj[nm] = (q, kf, kb, vf, vb, lf[:, :nh], gs)
            q, kf, kb, vf, vb, lf, mul_p = proj["p"]
            lf3 = lf.reshape(bsz, seq, nh)
            op = _fox_attn_prompt(q, kb, vb, _cumsum_rows(lf3) * LOG2E, bsz, seq, hd)
            b_out["kp"].append(kf.reshape(bsz, seq, nh, hd))
            b_out["vp"].append(vf.reshape(bsz, seq, nh, hd))
            b_out["lp"].append(lf3)
            q, kf, kb, vf, vb, lf, mul_s = proj["s"]
            past = cache_b_k.shape[2]
            lf3 = lf.reshape(dbsz, dseq, nh)
            lf_all = jnp.concatenate([cache_b_logf[j].astype(F32), lf3], axis=1)
            cum = _cumsum_rows(lf_all).transpose(0, 2, 1) * LOG2E
            os_ = _fox_attn_sample(q, kb, vb, cache_b_k.reshape(-1, dbsz, past, d),
                                   cache_b_v.reshape(-1, dbsz, past, d), j,
                                   cum[:, :, past:, None], cum[:, :, None, :past],
                                   cum[:, :, None, past:], dseq, hd)
            b_out["ks"].append(kf.reshape(dbsz, dseq, nh, hd))
            b_out["vs"].append(vf.reshape(dbsz, dseq, nh, hd))
            b_out["ls"].append(lf3)
            w_out = b_w_out
        xp, = _mm(op, w_out, j, 0, d, epi="resid", mul=mul_p, xres=xp, gate=mod_p[2],
                  rows_per_seq=seq, out_dtypes=(F32,), name="out_proj")
        xs, = _mm(os_, w_out, j, 0, d, epi="resid", mul=mul_s, xres=xs, gate=mod_s[2],
                  rows_per_seq=ms, out_dtypes=(F32,), name="out_proj")

        cnt0 = jnp.zeros((ne, LANE), F32)
        h2p, h3p, eip, ewp, rkp, cnt1 = _router(xp, n2g, i, mod_p[3], mod_p[4], seq, router_wt, router_b3, cnt0)
        h2s, h3s, eis, ews, rks, cnt2 = _router(xs, n2g, i, mod_s[3], mod_s[4], ms, router_wt, router_b3, cnt1)
        h3 = jnp.concatenate([h3p, h3s], axis=0)
        ei = jnp.concatenate([eip, eis], axis=1)
        rank = jnp.concatenate([rkp, rks], axis=1)
        slot_token, pos, tile_expert, n_valid = _moe_plan(ei, rank, cnt2[:, 0].astype(jnp.int32), MOE_TILE)
        ys = _experts(h3, slot_token, tile_expert, n_valid, exp_w_gate, exp_w_up, exp_w_down, i)
        sp, = _mm(h2p, sh_gu, i, 0, sh_gu.shape[-1], epi="swiglu", out_dtypes=(BF16,), name="shared_up")
        ss, = _mm(h2s, sh_gu, i, 0, sh_gu.shape[-1], epi="swiglu", out_dtypes=(BF16,), name="shared_up")
        xp = _combine(xp, ys, pos[:, :mp], ewp.T, sp, sh_w_down, i, mod_p[5], seq)
        xs = _combine(xs, ys, pos[:, mp:], ews.T, ss, sh_w_down, i, mod_s[5], ms)

    st = jnp.stack
    return (xp.reshape(bsz, seq, d), xs.reshape(dbsz, dseq, d),
            st(a_out["kp"]), st(a_out["vp"]), st(a_out["ks"]), st(a_out["vs"]),
            st(b_out["kp"]), st(b_out["vp"]), st(b_out["lp"]),
            st(b_out["ks"]), st(b_out["vs"]), st(b_out["ls"]))
```

```python
import functools

import jax
import jax.numpy as jnp
import numpy as np
from jax import lax
from jax.experimental import pallas as pl
from jax.experimental.pallas import tpu as pltpu

F32 = jnp.float32
BF16 = jnp.bfloat16

CHUNK = 64
N_PREV_CHUNKS = 8
TOP_K = 8
N_GROUPS = 8
TOPK_GROUPS = 4
ROUTED_SCALE = 2.5
NORM_EPS = 1e-6
NEG_INF = -1e30
REMOVED = -3e38
LANE = 128
VMEM_LIMIT = 56 * 1024 * 1024
Q_TILE = 128
BAND_KEYS = (N_PREV_CHUNKS + 2) * CHUNK
FOX_TILE = 256
MOE_TILE = 256
LOG2E = 1.4426950408889634
CHUNK_HEADS = 4
FOX_HEADS = 4


def _pick(n, cands):
    for c in cands:
        if n % c == 0:
            return c
    return n


def _params(sem):
    return pltpu.CompilerParams(dimension_semantics=sem, vmem_limit_bytes=VMEM_LIMIT)


def _nt(a, b):
    return lax.dot_general(a, b, (((1,), (1,)), ((), ())), preferred_element_type=F32)


def _dot(a, b):
    return jnp.dot(a, b, preferred_element_type=F32)


def _mod_spec(mod, tm, tn, rows_per_seq, row_arg, col_arg):
    if mod.ndim == 3:
        def imap(*g):
            col = 0 if col_arg is None else g[col_arg]
            return ((g[row_arg] * tm) // rows_per_seq, 0, col)
        return pl.BlockSpec((None, 1, tn), imap)

    def imap2(*g):
        col = 0 if col_arg is None else g[col_arg]
        return (g[row_arg], col)
    return pl.BlockSpec((tm, tn), imap2)


def _adaln_kernel(c_ref, w_ref, b_ref, o_ref):
    c = c_ref[...]
    a = (c * jax.nn.sigmoid(c)).astype(BF16)
    o_ref[...] = _dot(a, w_ref[...].astype(BF16)) + b_ref[...]


def _adaln(c_all, ada_w, ada_b3, layer):
    r, d = c_all.shape
    n = ada_w.shape[2]
    tn = _pick(n, (1024, 512, 256, 128))
    return pl.pallas_call(
        _adaln_kernel,
        grid=(n // tn,),
        in_specs=[pl.BlockSpec((r, d), lambda j: (0, 0)),
                  pl.BlockSpec((None, d, tn), lambda j: (layer, 0, j)),
                  pl.BlockSpec((None, 1, tn), lambda j: (layer, 0, j))],
        out_specs=pl.BlockSpec((r, tn), lambda j: (0, j)),
        out_shape=jax.ShapeDtypeStruct((r, n), F32),
        compiler_params=_params(("arbitrary",)),
        name="adaln",
    )(c_all, ada_w, ada_b3)


def _modulated(x, g, sh, sc):
    ms = jnp.mean(x * x, axis=-1, keepdims=True)
    y = x * lax.rsqrt(ms + NORM_EPS) * g
    return y * (1.0 + sc) + sh


def _modulate_kernel(x_ref, g_ref, sh_ref, sc_ref, o_ref):
    o_ref[...] = _modulated(x_ref[...], g_ref[...], sh_ref[...], sc_ref[...]).astype(BF16)


def _modulate(x, g3, layer, shift, scale, rows_per_seq):
    m, d = x.shape
    tm = _pick(min(m, rows_per_seq), (512, 256, 128, 64, 32, 16, 8))
    return pl.pallas_call(
        _modulate_kernel,
        grid=(m // tm,),
        in_specs=[pl.BlockSpec((tm, d), lambda i: (i, 0)),
                  pl.BlockSpec((None, 1, d), lambda i: (layer, 0, 0)),
                  _mod_spec(shift, tm, d, rows_per_seq, 0, None),
                  _mod_spec(scale, tm, d, rows_per_seq, 0, None)],
        out_specs=pl.BlockSpec((tm, d), lambda i: (i, 0)),
        out_shape=jax.ShapeDtypeStruct((m, d), BF16),
        compiler_params=_params(("arbitrary",)),
        name="modulate",
    )(x, g3, shift, scale)


def _mm_kernel(*refs, epi, has_mul, n_out, scale):
    it = iter(refs)
    a_ref = next(it)
    m_ref = next(it) if has_mul else None
    w_ref = next(it)
    extra = []
    n_extra = {"headnorm": 1, "plain": 0, "logsig": 1, "sigmoid": 0, "resid": 2, "swiglu": 0}[epi]
    for _ in range(n_extra):
        extra.append(next(it))
    outs = [next(it) for _ in range(n_out)]
    wb_ref = next(it)

    @pl.when(pl.program_id(1) == 0)
    def _():
        wb_ref[...] = w_ref[...].astype(BF16)

    a = a_ref[...]
    if has_mul:
        a = a * m_ref[...]
    acc = _dot(a, wb_ref[...])

    if epi == "headnorm":
        g = extra[0][...]
        hd = g.shape[-1]
        ys = []
        for h in range(acc.shape[1] // hd):
            blk = acc[:, h * hd:(h + 1) * hd]
            ms = jnp.mean(blk * blk, axis=-1, keepdims=True)
            ys.append(blk * lax.rsqrt(ms + NORM_EPS) * g)
        res = jnp.concatenate(ys, axis=1)
    elif epi == "plain":
        res = acc
    elif epi == "logsig":
        z = acc + extra[0][...]
        res = jnp.minimum(z, 0.0) - jnp.log1p(jnp.exp(-jnp.abs(z)))
    elif epi == "sigmoid":
        res = jax.nn.sigmoid(acc)
    elif epi == "resid":
        res = extra[0][...] + extra[1][...] * acc
    elif epi == "swiglu":
        f = acc.shape[1] // 2
        ga = acc[:, :f]
        res = ga * jax.nn.sigmoid(ga) * acc[:, f:]
    for o in outs:
        val = res * scale if (o.dtype == BF16 and scale != 1.0) else res
        if len(o.shape) == 3:
            val = val.reshape(o.shape)
        o[...] = val.astype(o.dtype)


def _mm(a, w, layer, col0, n, *, epi, out_dtypes, mul=None, g=None, bias=None, xres=None,
        gate=None, rows_per_seq=None, scale=1.0, head_dim=None, name="mm"):
    m, k = a.shape
    if epi == "swiglu":
        tn = n
    else:
        tn = _pick(n, (1024, 512, 256, 128))
    assert col0 % tn == 0
    cb = col0 // tn
    tm = _pick(m, (512, 256, 128))
    if rows_per_seq is not None:
        assert rows_per_seq % tm == 0 or (gate is not None and gate.ndim == 2)
    n_out_cols = n // 2 if epi == "swiglu" else n
    tno = tn // 2 if epi == "swiglu" else tn

    args = [a]
    specs = [pl.BlockSpec((tm, k), lambda j, i: (i, 0))]
    if mul is not None:
        args.append(mul)
        specs.append(pl.BlockSpec((tm, k), lambda j, i: (i, 0)))
    args.append(w)
    specs.append(pl.BlockSpec((None, k, tn), lambda j, i: (layer, 0, cb + j)))
    if epi == "headnorm":
        args.append(g)
        specs.append(pl.BlockSpec((None, 1, g.shape[-1]), lambda j, i: (layer, 0, 0)))
    elif epi == "logsig":
        args.append(bias)
        specs.append(pl.BlockSpec((None, 1, tn), lambda j, i: (layer, 0, j)))
    elif epi == "resid":
        args.append(xres)
        specs.append(pl.BlockSpec((tm, tn), lambda j, i: (i, j)))
        args.append(gate)
        specs.append(_mod_spec(gate, tm, tn, rows_per_seq, 1, 0))
    out_shape, out_specs = [], []
    for dt in out_dtypes:
        if head_dim is not None and dt == F32:
            out_shape.append(jax.ShapeDtypeStruct((m, n_out_cols // head_dim, head_dim), dt))
            out_specs.append(pl.BlockSpec((tm, tno // head_dim, head_dim), lambda j, i: (i, j, 0)))
        else:
            out_shape.append(jax.ShapeDtypeStruct((m, n_out_cols), dt))
            out_specs.append(pl.BlockSpec((tm, tno), lambda j, i: (i, j)))
    res = pl.pallas_call(
        functools.partial(_mm_kernel, epi=epi, has_mul=mul is not None, n_out=len(out_dtypes),
                          scale=scale),
        grid=(n // tn, m // tm),
        in_specs=specs,
        out_specs=out_specs,
        out_shape=out_shape,
        scratch_shapes=[pltpu.VMEM((k, tn), BF16)],
        compiler_params=_params(("arbitrary", "arbitrary")),
        name=name,
    )(*args)
    return res


def _chunk_prompt_kernel(q_ref, k_ref, v_ref, b_ref, o_ref, *, n_tiles, edge, hd):
    heads = q_ref.shape[1] // hd

    def body(m, carry):
        q0 = pl.multiple_of(m * Q_TILE, Q_TILE)
        k0 = pl.multiple_of(jnp.maximum(m - edge, 0) * Q_TILE, Q_TILE)
        bi = jnp.minimum(m, edge)
        sls = [slice(g * hd, (g + 1) * hd) for g in range(heads)]
        ss = [_nt(q_ref[pl.ds(q0, Q_TILE), sl], k_ref[pl.ds(k0, BAND_KEYS), sl]) for sl in sls]
        ps, ls = [], []
        for g in range(heads):
            s = ss[g] + b_ref[g, bi]
            mx = jnp.max(s, axis=-1, keepdims=True)
            p = jnp.exp2(s - mx)
            ls.append(jnp.sum(p, axis=-1, keepdims=True))
            ps.append(p.astype(BF16))
        for g in range(heads):
            o = _dot(ps[g], v_ref[pl.ds(k0, BAND_KEYS), sls[g]]) / ls[g]
            o_ref[pl.ds(q0, Q_TILE), sls[g]] = o.astype(BF16)
        return carry
    lax.fori_loop(0, n_tiles, body, 0)


def _chunk_bias_prompt(table):
    nh = table.shape[0]
    rel_max = (table.shape[1] - 1) // 2
    edge = (BAND_KEYS - Q_TILE) // Q_TILE
    mm = np.arange(edge + 1)[:, None, None]
    qpos = mm * Q_TILE + np.arange(Q_TILE)[None, :, None]
    kpos = np.maximum(mm - edge, 0) * Q_TILE + np.arange(BAND_KEYS)[None, None, :]
    qc, kc = qpos // CHUNK, kpos // CHUNK
    vis = (kc <= qc) & (kc >= qc - N_PREV_CHUNKS)
    period = Q_TILE + BAND_KEYS - 1
    c_m = np.minimum(np.arange(edge + 1), edge) * Q_TILE
    n = np.arange(period)
    n = np.where(n < BAND_KEYS, n, n - period)[None, :]
    idx = np.clip(c_m[:, None] - n, -rel_max, rel_max) + rel_max
    w = table[:, idx].astype(F32) * LOG2E
    skew = jnp.tile(w, (1, 1, Q_TILE))[..., :Q_TILE * (period - 1)]
    b = skew.reshape(nh, edge + 1, Q_TILE, period - 1)[..., :BAND_KEYS]
    return jnp.where(vis[None], b, NEG_INF), edge


def _chunk_attn_prompt(q, k, v, bias, edge, bsz, seq, hd):
    d = q.shape[1]
    nh = d // hd
    heads = _pick(nh, (CHUNK_HEADS, 2, 1))
    q3, k3, v3 = (t.reshape(bsz, seq, d) for t in (q, k, v))
    spec = pl.BlockSpec((None, seq, heads * hd), lambda b, h: (b, 0, h))
    o = pl.pallas_call(
        functools.partial(_chunk_prompt_kernel, n_tiles=seq // Q_TILE, edge=edge, hd=hd),
        grid=(bsz, nh // heads),
        in_specs=[spec, spec, spec,
                  pl.BlockSpec((heads, edge + 1, Q_TILE, BAND_KEYS), lambda b, h: (h, 0, 0, 0))],
        out_specs=spec,
        out_shape=jax.ShapeDtypeStruct((bsz, seq, d), BF16),
        compiler_params=_params(("arbitrary", "arbitrary")),
        name="chunk_attn_prompt",
    )(q3, k3, v3, bias)
    return o.reshape(bsz * seq, d)


def _sample_attn_kernel(*refs, hd, fox):
    if fox:
        q_ref, kn_ref, vn_ref, kc_ref, vc_ref, cq_ref, ckc_ref, ckn_ref, o_ref, ks_scr, vs_scr = refs
    else:
        q_ref, kn_ref, vn_ref, kc_ref, vc_ref, bc_ref, bn_ref, o_ref, ks_scr, vs_scr = refs
    t = q_ref.shape[0]
    nh = q_ref.shape[1] // hd
    past = kc_ref.shape[0]
    rows = _pick(past, (512, 256, 128))
    for c in range(past // rows):
        rs = slice(c * rows, (c + 1) * rows)
        ks_scr[rs, :] = kc_ref[rs].reshape(rows, nh * hd).astype(BF16)
        vs_scr[rs, :] = vc_ref[rs].reshape(rows, nh * hd).astype(BF16)
    if fox:
        row = lax.broadcasted_iota(jnp.int32, (t, t), 0)
        col = lax.broadcasted_iota(jnp.int32, (t, t), 1)
        causal = col <= row
    for h in range(nh):
        sl = slice(h * hd, (h + 1) * hd)
        q = q_ref[:, sl]
        kc = ks_scr[:, sl]
        vc = vs_scr[:, sl]
        s1 = _nt(q, kc)
        s2 = _nt(q, kn_ref[:, sl])
        if fox:
            cq = cq_ref[h]
            s1 = s1 + cq - ckc_ref[h]
            s2 = jnp.where(causal, s2 + cq - ckn_ref[h], NEG_INF)
        else:
            s1 = s1 + bc_ref[h]
            s2 = s2 + bn_ref[h]
        mx = jnp.maximum(jnp.max(s1, axis=-1, keepdims=True), jnp.max(s2, axis=-1, keepdims=True))
        p1 = jnp.exp2(s1 - mx)
        p2 = jnp.exp2(s2 - mx)
        l = jnp.sum(p1, axis=-1, keepdims=True) + jnp.sum(p2, axis=-1, keepdims=True)
        o = (_dot(p1.astype(BF16), vc) + _dot(p2.astype(BF16), vn_ref[:, sl])) / l
        o_ref[:, sl] = o.astype(BF16)


def _chunk_attn_sample(q, kn, vn, cache_k, cache_v, layer, bias_c, bias_n, t, hd):
    _, bsz, w, nh, _ = cache_k.shape
    d = nh * hd
    hg = _pick(nh, (8,)) * hd
    ng = d // hg
    tok = pl.BlockSpec((t, hg), lambda b, g: (b, g))
    cache = pl.BlockSpec((None, None, w, hg // hd, hd), lambda b, g: (layer, b, 0, g, 0))
    return pl.pallas_call(
        functools.partial(_sample_attn_kernel, hd=hd, fox=False),
        grid=(bsz, ng),
        in_specs=[tok, tok, tok, cache, cache,
                  pl.BlockSpec((hg // hd, t, w), lambda b, g: (g, 0, 0)),
                  pl.BlockSpec((hg // hd, t, t), lambda b, g: (g, 0, 0))],
        out_specs=tok,
        out_shape=jax.ShapeDtypeStruct((bsz * t, d), BF16),
        scratch_shapes=[pltpu.VMEM((w, hg), BF16), pltpu.VMEM((w, hg), BF16)],
        compiler_params=_params(("arbitrary", "arbitrary")),
        name="chunk_attn_sample",
    )(q, kn, vn, cache_k, cache_v, bias_c, bias_n)


def _fox_attn_sample(q, kn, vn, cache_k, cache_v, layer, cq, ckc, ckn, t, hd):
    _, bsz, w, nh, _ = cache_k.shape
    d = nh * hd
    hg = _pick(nh, (8,)) * hd
    ng = d // hg
    gh = hg // hd
    tok = pl.BlockSpec((t, hg), lambda b, g: (b, g))
    cache = pl.BlockSpec((None, None, w, gh, hd), lambda b, g: (layer, b, 0, g, 0))
    return pl.pallas_call(
        functools.partial(_sample_attn_kernel, hd=hd, fox=True),
        grid=(bsz, ng),
        in_specs=[tok, tok, tok, cache, cache,
                  pl.BlockSpec((None, gh, t, 1), lambda b, g: (b, g, 0, 0)),
                  pl.BlockSpec((None, gh, 1, w), lambda b, g: (b, g, 0, 0)),
                  pl.BlockSpec((None, gh, 1, t), lambda b, g: (b, g, 0, 0))],
        out_specs=tok,
        out_shape=jax.ShapeDtypeStruct((bsz * t, d), BF16),
        scratch_shapes=[pltpu.VMEM((w, hg), BF16), pltpu.VMEM((w, hg), BF16)],
        compiler_params=_params(("arbitrary", "arbitrary")),
        name="fox_attn_sample",
    )(q, kn, vn, cache_k, cache_v, cq, ckc, ckn)


def _fox_prompt_kernel(q_ref, k_ref, v_ref, cq_ref, ck_ref, o_ref, *scr):
    i = pl.program_id(2)
    tq = q_ref.shape[0]
    heads = cq_ref.shape[0]
    hd = q_ref.shape[1] // heads
    m_scr, l_scr, acc_scr = scr[:heads], scr[heads:2 * heads], scr[2 * heads:3 * heads]
    qt_scr = scr[3 * heads:]
    row = lax.broadcasted_iota(jnp.int32, (tq, tq), 0)
    col = lax.broadcasted_iota(jnp.int32, (tq, tq), 1)
    for g in range(heads):
        m_scr[g][...] = jnp.full((1, tq), NEG_INF, F32)
        l_scr[g][...] = jnp.zeros((1, tq), F32)
        acc_scr[g][...] = jnp.zeros(acc_scr[g].shape, F32)
        qt_scr[g][...] = q_ref[:, g * hd:(g + 1) * hd].astype(F32).T.astype(BF16)

    def step(j, diag):
        k0 = pl.multiple_of(j * tq, tq)
        sts = [_dot(k_ref[pl.ds(k0, tq), g * hd:(g + 1) * hd], qt_scr[g][...]) for g in range(heads)]
        pts, alphas = [], []
        for g in range(heads):
            ck = ck_ref[g, pl.ds(k0, tq), :]
            st = sts[g] + cq_ref[g] - jnp.concatenate([ck] * (tq // LANE), axis=1)
            if diag:
                st = jnp.where(row <= col, st, NEG_INF)
            m_old = m_scr[g][...]
            m_new = jnp.maximum(m_old, jnp.max(st, axis=0, keepdims=True))
            alpha = jnp.exp2(m_old - m_new)
            pt = jnp.exp2(st - m_new)
            l_scr[g][...] = alpha * l_scr[g][...] + jnp.sum(pt, axis=0, keepdims=True)
            m_scr[g][...] = m_new
            pts.append(pt.astype(BF16))
            alphas.append(alpha)
        for g in range(heads):
            vb = v_ref[pl.ds(k0, tq), g * hd:(g + 1) * hd]
            pv = lax.dot_general(vb, pts[g], (((0,), (0,)), ((), ())), preferred_element_type=F32)
            acc_scr[g][...] = alphas[g] * acc_scr[g][...] + pv

    def body(j, carry):
        step(j, False)
        return carry
    lax.fori_loop(0, i, body, 0)
    step(i, True)
    for g in range(heads):
        o_ref[:, g * hd:(g + 1) * hd] = (acc_scr[g][...] / l_scr[g][...]).T.astype(BF16)


def _cumsum_rows(x):
    b, s, h = x.shape
    pad = -s % LANE
    xb = jnp.pad(x, ((0, 0), (0, pad), (0, 0))).reshape(b, (s + pad) // LANE, LANE, h)
    tri = jnp.tril(jnp.ones((LANE, LANE), F32))
    within = jnp.einsum("ij,bnjh->bnih", tri, xb, precision=lax.Precision.HIGHEST)
    tot = within[:, :, -1, :]
    offs = jnp.cumsum(tot, axis=1) - tot
    return (within + offs[:, :, None, :]).reshape(b, s + pad, h)[:, :s]


def _fox_attn_prompt(q, k, v, cum, bsz, seq, hd):
    d = q.shape[1]
    nh = d // hd
    heads = _pick(nh, (FOX_HEADS, 2, 1))
    tq = _pick(seq, (FOX_TILE, 128))
    cum_t = cum.transpose(0, 2, 1)
    cq = cum_t[:, :, None, :]
    ck = jnp.broadcast_to(cum_t[..., None], (bsz, nh, seq, LANE))
    q3, k3, v3 = (t.reshape(bsz, seq, d) for t in (q, k, v))
    qspec = pl.BlockSpec((None, tq, heads * hd), lambda b, h, i: (b, i, h))
    kspec = pl.BlockSpec((None, seq, heads * hd), lambda b, h, i: (b, 0, h))
    o = pl.pallas_call(
        _fox_prompt_kernel,
        grid=(bsz, nh // heads, seq // tq),
        in_specs=[qspec, kspec, kspec,
                  pl.BlockSpec((None, heads, 1, tq), lambda b, h, i: (b, h, 0, i)),
                  pl.BlockSpec((None, heads, seq, LANE), lambda b, h, i: (b, h, 0, 0))],
        out_specs=qspec,
        out_shape=jax.ShapeDtypeStruct((bsz, seq, d), BF16),
        scratch_shapes=([pltpu.VMEM((1, tq), F32)] * (2 * heads)
                        + [pltpu.VMEM((hd, tq), F32)] * heads
                        + [pltpu.VMEM((hd, tq), BF16)] * heads),
        compiler_params=_params(("arbitrary", "arbitrary", "arbitrary")),
        name="fox_attn_prompt",
    )(q3, k3, v3, cq, ck)
    return o.reshape(bsz * seq, d)


def _router_kernel(x_ref, g_ref, sh_ref, sc_ref, wt_ref, rb_ref, cin_ref,
                   h_ref, h3_ref, ei_ref, ew_ref, rk_ref, cout_ref, cnt):
    @pl.when(pl.program_id(0) == 0)
    def _():
        cnt[...] = cin_ref[...]

    hf = _modulated(x_ref[...], g_ref[...], sh_ref[...], sc_ref[...])
    hh = hf.astype(BF16)
    h_ref[...] = hh
    h3_ref[...] = hf.reshape(h3_ref.shape).astype(BF16)
    hl = (hf - hh.astype(F32)).astype(BF16)
    w = wt_ref[...]
    wh = w.astype(BF16)
    wl = (w - wh.astype(F32)).astype(BF16)
    logits = _nt(wh, hh) + (_nt(wh, hl) + _nt(wl, hh))
    scores = jax.nn.sigmoid(logits)
    sel = scores + rb_ref[...]
    ne, tm = sel.shape
    gsz = ne // N_GROUPS
    row_g = lax.broadcasted_iota(jnp.int32, (gsz, tm), 0).astype(F32)
    row_n = lax.broadcasted_iota(jnp.int32, (N_GROUPS, tm), 0).astype(F32)
    row_e = lax.broadcasted_iota(jnp.int32, (ne, tm), 0).astype(F32)

    def first_max(x, rows, big):
        mx = jnp.max(x, axis=0, keepdims=True)
        ix = jnp.min(jnp.where(x == mx, rows, big), axis=0, keepdims=True)
        return mx, ix

    grp = jnp.zeros((N_GROUPS, tm), F32)
    for g in range(N_GROUPS):
        blk = sel[g * gsz:(g + 1) * gsz, :]
        m1, i1 = first_max(blk, row_g, float(gsz))
        m2 = jnp.max(jnp.where(row_g == i1, REMOVED, blk), axis=0, keepdims=True)
        grp = jnp.where(row_n == float(g), m1 + m2, grp)
    gsel = jnp.zeros((N_GROUPS, tm), F32)
    work = grp
    for _ in range(TOPK_GROUPS):
        _, ix = first_max(work, row_n, float(N_GROUPS))
        hit = row_n == ix
        gsel = jnp.where(hit, 1.0, gsel)
        work = jnp.where(hit, REMOVED, work)
    work = jnp.concatenate(
        [jnp.where(gsel[g:g + 1, :] > 0.0, sel[g * gsz:(g + 1) * gsz, :], NEG_INF)
         for g in range(N_GROUPS)], axis=0)
    row_k = lax.broadcasted_iota(jnp.int32, (TOP_K, tm), 0)
    ei = jnp.zeros((TOP_K, tm), F32)
    ew = jnp.zeros((TOP_K, tm), F32)
    hits = []
    chosen = jnp.zeros((ne, tm), F32)
    for r in range(TOP_K):
        _, ix = first_max(work, row_e, float(ne))
        hit = row_e == ix
        hits.append(hit)
        chosen = jnp.where(hit, 1.0, chosen)
        wr = jnp.sum(jnp.where(hit, scores, 0.0), axis=0, keepdims=True)
        ei = jnp.where(row_k == r, ix, ei)
        ew = jnp.where(row_k == r, wr, ew)
        work = jnp.where(hit, REMOVED, work)
    ew = ew / jnp.sum(ew, axis=0, keepdims=True) * ROUTED_SCALE
    ei_ref[...] = ei.astype(jnp.int32)
    ew_ref[...] = ew
    cb = chosen.astype(BF16)
    tr = lax.broadcasted_iota(jnp.int32, (tm, tm), 0)
    tc = lax.broadcasted_iota(jnp.int32, (tm, tm), 1)
    before = _dot(cb, jnp.where(tr < tc, 1.0, 0.0).astype(BF16))
    total = _dot(cb, jnp.ones((tm, LANE), BF16))
    base = cnt[...]
    rank_e = jnp.concatenate([base] * (tm // LANE), axis=1) + before
    rk = jnp.zeros((TOP_K, tm), F32)
    for r in range(TOP_K):
        rr = jnp.sum(jnp.where(hits[r], rank_e, 0.0), axis=0, keepdims=True)
        rk = jnp.where(row_k == r, rr, rk)
    rk_ref[...] = rk.astype(jnp.int32)
    cnt[...] = base + total
    cout_ref[...] = base + total


def _router(x, g3, layer, shift, scale, rows_per_seq, router_wt, router_b3, cnt_in):
    m, d = x.shape
    ne = router_wt.shape[1]
    tm = _pick(m, (256, 128))
    asg = pl.BlockSpec((TOP_K, tm), lambda i: (0, i))
    return pl.pallas_call(
        _router_kernel,
        grid=(m // tm,),
        in_specs=[pl.BlockSpec((tm, d), lambda i: (i, 0)),
                  pl.BlockSpec((None, 1, d), lambda i: (layer, 0, 0)),
                  _mod_spec(shift, tm, d, rows_per_seq, 0, None),
                  _mod_spec(scale, tm, d, rows_per_seq, 0, None),
                  pl.BlockSpec((None, ne, d), lambda i: (layer, 0, 0)),
                  pl.BlockSpec((None, ne, 1), lambda i: (layer, 0, 0)),
                  pl.BlockSpec((ne, LANE), lambda i: (0, 0))],
        out_specs=[pl.BlockSpec((tm, d), lambda i: (i, 0)),
                   pl.BlockSpec((tm, d // LANE, LANE), lambda i: (i, 0, 0)), asg, asg, asg,
                   pl.BlockSpec((ne, LANE), lambda i: (0, 0))],
        out_shape=[jax.ShapeDtypeStruct((m, d), BF16),
                   jax.ShapeDtypeStruct((m, d // LANE, LANE), BF16),
                   jax.ShapeDtypeStruct((TOP_K, m), jnp.int32),
                   jax.ShapeDtypeStruct((TOP_K, m), F32),
                   jax.ShapeDtypeStruct((TOP_K, m), jnp.int32),
                   jax.ShapeDtypeStruct((ne, LANE), F32)],
        scratch_shapes=[pltpu.VMEM((ne, LANE), F32)],
        compiler_params=_params(("arbitrary",)),
        name="router",
    )(x, g3, shift, scale, router_wt, router_b3, cnt_in)


def _experts_kernel(te_ref, nv_ref, tok_ref, nxt_ref, h_ref, wg_ref, wu_ref, wd_ref, y_ref,
                    wgb, wub, wdb, xbuf, sem):
    t = pl.program_id(0)
    nv = nv_ref[0]
    tm = xbuf.shape[1]
    d = wg_ref.shape[0]
    slot = lax.rem(t, 2)
    e = te_ref[t]
    prev = te_ref[jnp.maximum(t - 1, 0)]

    def row_copy(idx_ref, r, s):
        return pltpu.make_async_copy(h_ref.at[idx_ref[0, r]], xbuf.at[s, r], sem.at[s])

    @pl.when(t == 0)
    def _():
        for r in range(tm):
            row_copy(tok_ref, r, 0).start(priority=r % 2)

    @pl.when((t == 0) | (e != prev))
    def _():
        wgb[...] = wg_ref[...].astype(BF16)
        wub[...] = wu_ref[...].astype(BF16)
        wdb[...] = wd_ref[...].astype(BF16)

    @pl.when(t < nv)
    def _():
        for r in range(tm):
            row_copy(tok_ref, r, slot).wait()
        for r in range(tm):
            row_copy(nxt_ref, r, 1 - slot).start(priority=r % 2)
        x = xbuf[slot].astype(F32).reshape(tm, d).astype(BF16)
        a = _dot(x, wgb[...])
        b = _dot(x, wub[...])
        s = (a * jax.nn.sigmoid(a) * b).astype(BF16)
        y_ref[...] = _dot(s, wdb[...]).reshape(y_ref.shape).astype(BF16)

    @pl.when(t == nv - 1)
    def _():
        for r in range(tm):
            row_copy(nxt_ref, r, 1 - slot).wait()

    @pl.when(t >= nv)
    def _():
        y_ref[...] = jnp.zeros(y_ref.shape, BF16)


def _experts(h3, slot_token, tile_expert, n_valid, wg, wu, wd, layer):
    _, nl, lane = h3.shape
    d = nl * lane
    f = wg.shape[-1]
    tm = MOE_TILE
    n_tiles = slot_token.shape[0] // tm
    tok3 = slot_token.reshape(n_tiles, 1, tm)

    def tok_spec(shift):
        return pl.BlockSpec((None, 1, tm), lambda t, te, nv: (jnp.minimum(t + shift, nv[0] - 1), 0, 0),
                            memory_space=pltpu.SMEM)
    grid_spec = pltpu.PrefetchScalarGridSpec(
        num_scalar_prefetch=2,
        grid=(n_tiles,),
        in_specs=[tok_spec(0), tok_spec(1),
                  pl.BlockSpec(memory_space=pl.ANY),
                  pl.BlockSpec((None, None, d, f), lambda t, te, nv: (layer, te[t], 0, 0)),
                  pl.BlockSpec((None, None, d, f), lambda t, te, nv: (layer, te[t], 0, 0)),
                  pl.BlockSpec((None, None, f, d), lambda t, te, nv: (layer, te[t], 0, 0))],
        out_specs=pl.BlockSpec((tm, nl, lane), lambda t, te, nv: (t, 0, 0)),
        scratch_shapes=[pltpu.VMEM((d, f), BF16), pltpu.VMEM((d, f), BF16), pltpu.VMEM((f, d), BF16),
                        pltpu.VMEM((2, tm, nl, lane), BF16), pltpu.SemaphoreType.DMA((2,))],
    )
    return pl.pallas_call(
        _experts_kernel,
        grid_spec=grid_spec,
        out_shape=jax.ShapeDtypeStruct((n_tiles * tm, nl, lane), BF16),
        compiler_params=_params(("arbitrary",)),
        name="experts",
    )(tile_expert, n_valid, tok3, tok3, h3, wg, wu, wd)


def _moe_plan(ei, rank, counts, tm):
    kk, t = ei.shape
    ne = counts.shape[0]
    n_asg = kk * t
    n_tiles = n_asg // tm + ne
    i32 = jnp.int32
    ar = jnp.arange(ne, dtype=i32)
    tiles_per = (counts + tm - 1) // tm
    tile_end = jnp.cumsum(tiles_per)
    pstart = (tile_end - tiles_per) * tm
    ustart = jnp.cumsum(counts) - counts
    pos = rank + jnp.sum(jnp.where(ei[..., None] == ar, pstart, 0), axis=-1)
    n_valid = tile_end[-1]
    tile_ids = jnp.arange(n_tiles, dtype=i32)
    te = jnp.sum((tile_end[None, :] <= tile_ids[:, None]).astype(i32), axis=1)
    te = jnp.minimum(te, jnp.sum((tile_end <= n_valid - 1).astype(i32)))
    onehot = te[:, None] == ar[None, :]
    t_pstart, t_count, t_ustart = (jnp.sum(jnp.where(onehot, v, 0), axis=1)
                                   for v in (pstart, counts, ustart))
    within = tile_ids[:, None] * tm + jnp.arange(tm, dtype=i32)[None, :] - t_pstart[:, None]
    valid = (within < t_count[:, None]) & (tile_ids[:, None] < n_valid)
    order = jnp.argsort(ei.T.reshape(-1), stable=True)
    src = jnp.take(order, jnp.clip(t_ustart[:, None] + within, 0, n_asg - 1).reshape(-1)) // kk
    slot_token = jnp.where(valid.reshape(-1), src, 0).astype(i32)
    return slot_token, pos.astype(i32), te.astype(i32), n_valid.reshape(1).astype(i32)


def _combine_kernel(pos_ref, x_ref, ys_ref, w_ref, sa_ref, sd_ref, gate_ref, o_ref, sdb, ybuf, sem):
    @pl.when(pl.program_id(0) == 0)
    def _():
        sdb[...] = sd_ref[...].astype(BF16)

    tm, d = x_ref.shape

    def row_copy(k, r):
        return pltpu.make_async_copy(ys_ref.at[pos_ref[k, r]], ybuf.at[k * tm + r], sem)

    def issue(r, carry):
        for k in range(TOP_K):
            row_copy(k, r).start(priority=k % 2)
        return carry
    lax.fori_loop(0, tm, issue, 0)

    w = w_ref[...]
    acc = _dot(sa_ref[...], sdb[...])

    for r in range(tm):
        for k in range(TOP_K):
            row_copy(k, r).wait()

    for k in range(TOP_K):
        yk = ybuf[pl.ds(k * tm, tm)].astype(F32).reshape(tm, d)
        acc = acc + w[:, k:k + 1] * yk
    o_ref[...] = x_ref[...] + gate_ref[...] * acc


def _combine(x, ys, pos, w, sact, sh_w_down, layer, gate, rows_per_seq):
    m, d = x.shape
    f = sact.shape[1]
    tm = _pick(m, (256, 128))
    return pl.pallas_call(
        _combine_kernel,
        grid=(m // tm,),
        in_specs=[pl.BlockSpec((TOP_K, tm), lambda i: (0, i), memory_space=pltpu.SMEM),
                  pl.BlockSpec((tm, d), lambda i: (i, 0)),
                  pl.BlockSpec(memory_space=pl.ANY),
                  pl.BlockSpec((tm, TOP_K), lambda i: (i, 0)),
                  pl.BlockSpec((tm, f), lambda i: (i, 0)),
                  pl.BlockSpec((None, f, d), lambda i: (layer, 0, 0)),
                  _mod_spec(gate, tm, d, rows_per_seq, 0, None)],
        out_specs=pl.BlockSpec((tm, d), lambda i: (i, 0)),
        out_shape=jax.ShapeDtypeStruct((m, d), F32),
        scratch_shapes=[pltpu.VMEM((f, d), BF16),
                        pltpu.VMEM((TOP_K * tm, d // LANE, LANE), BF16),
                        pltpu.SemaphoreType.DMA],
        compiler_params=_params(("arbitrary",)),
        name="combine",
    )(pos, x, ys, w, sact, sh_w_down, gate)


def kernel(x_prompt, x_sample, c_prompt, c_sample, cache_a_k, cache_a_v, cache_b_k, cache_b_v,
           cache_b_logf, ada_w, ada_b, norm1_g, norm2_g, a_w_in, a_q_g, a_k_g, a_rel_table, a_w_out,
           b_w_in, b_f_bias, b_q_g, b_k_g, b_w_out, router_w, router_bias, exp_w_gate, exp_w_up,
           exp_w_down, sh_w_gate, sh_w_up, sh_w_down):
    bsz, seq, d = x_prompt.shape
    dbsz, dseq, _ = x_sample.shape
    depth = ada_w.shape[0]
    hd = a_q_g.shape[-1]
    nh = d // hd
    ne = router_w.shape[-1]
    mp, ms = bsz * seq, dbsz * dseq
    attn_scale = hd ** -0.5 * LOG2E

    xp = x_prompt.reshape(mp, d)
    xs = x_sample.reshape(ms, d)
    c_all = jnp.concatenate([c_prompt, c_sample], axis=0)
    ada_b3 = ada_b.reshape(depth, 1, -1)
    n1g = norm1_g.reshape(depth, 1, d)
    n2g = norm2_g.reshape(depth, 1, d)
    router_wt = jnp.swapaxes(router_w, 1, 2)
    router_b3 = router_bias.reshape(depth, ne, 1)
    sh_gu = jnp.concatenate([sh_w_gate, sh_w_up], axis=-1)

    a_out = {n: [] for n in ("kp", "vp", "ks", "vs")}
    b_out = {n: [] for n in ("kp", "vp", "lp", "ks", "vs", "ls")}

    for i in range(depth):
        mods = _adaln(c_all, ada_w, ada_b3, i)
        mod6 = jnp.split(mods, 6, axis=-1)
        mod_p = [t[:bsz].reshape(bsz, 1, d) for t in mod6]
        mod_s = [jnp.repeat(t[bsz:], dseq, axis=0) for t in mod6]
        hp = _modulate(xp, n1g, i, mod_p[0], mod_p[1], seq)
        hs = _modulate(xs, n1g, i, mod_s[0], mod_s[1], ms)
        j = i // 2
        if i % 2 == 0:
            qg, kg = a_q_g.reshape(-1, 1, hd), a_k_g.reshape(-1, 1, hd)
            proj = {}
            for nm, h_in in (("p", hp), ("s", hs)):
                q, = _mm(h_in, a_w_in, j, 0, d, epi="headnorm", g=qg, scale=attn_scale,
                         out_dtypes=(BF16,), name="a_q")
                kf, kb = _mm(h_in, a_w_in, j, d, d, epi="headnorm", g=kg,
                             out_dtypes=(F32, BF16), head_dim=hd, name="a_k")
                vf, vb = _mm(h_in, a_w_in, j, 2 * d, d, epi="plain",
                             out_dtypes=(F32, BF16), head_dim=hd, name="a_v")
                proj[nm] = (q, kf, kb, vf, vb)
            q, kf, kb, vf, vb = proj["p"]
            bias_p, edge = _chunk_bias_prompt(a_rel_table[j])
            op = _chunk_attn_prompt(q, kb, vb, bias_p, edge, bsz, seq, hd)
            keep = min(N_PREV_CHUNKS * CHUNK, seq)
            a_out["kp"].append(kf.reshape(bsz, seq, nh, hd)[:, seq - keep:])
            a_out["vp"].append(vf.reshape(bsz, seq, nh, hd)[:, seq - keep:])
            q, kf, kb, vf, vb = proj["s"]
            w = cache_a_k.shape[2]
            rel_max = (a_rel_table.shape[-1] - 1) // 2
            rel = np.arange(dseq)[:, None] + w - np.arange(w + dseq)[None, :]
            bias_s = a_rel_table[j][:, np.clip(rel, -rel_max, rel_max) + rel_max].astype(F32) * LOG2E
            os_ = _chunk_attn_sample(q, kb, vb, cache_a_k, cache_a_v, j,
                                     bias_s[:, :, :w], bias_s[:, :, w:], dseq, hd)
            a_out["ks"].append(kf.reshape(dbsz, dseq, nh, hd))
            a_out["vs"].append(vf.reshape(dbsz, dseq, nh, hd))
            mul_p = mul_s = None
            w_out = a_w_out
        else:
            qg, kg = b_q_g.reshape(-1, 1, hd), b_k_g.reshape(-1, 1, hd)
            w_f = jnp.pad(b_w_in[j][:, 3 * d:3 * d + nh], ((0, 0), (0, LANE - nh)))[None]
            f_b = jnp.pad(b_f_bias[j], (0, LANE - nh)).reshape(1, 1, LANE)
            w_g = b_w_in[j][:, 3 * d + nh:][None]
            proj = {}
            for nm, h_in in (("p", hp), ("s", hs)):
                q, = _mm(h_in, b_w_in, j, 0, d, epi="headnorm", g=qg, scale=attn_scale,
                         out_dtypes=(BF16,), name="b_q")
                kf, kb = _mm(h_in, b_w_in, j, d, d, epi="headnorm", g=kg,
                             out_dtypes=(F32, BF16), head_dim=hd, name="b_k")
                vf, vb = _mm(h_in, b_w_in, j, 2 * d, d, epi="plain",
                             out_dtypes=(F32, BF16), head_dim=hd, name="b_v")
                lf, = _mm(h_in, w_f, 0, 0, LANE, epi="logsig", bias=f_b, out_dtypes=(F32,),
                          name="b_logf")
                gs, = _mm(h_in, w_g, 0, 0, d, epi="sigmoid", out_dtypes=(BF16,), name="b_gate")
                proj[nm] = (q, kf, kb, vf, vb, lf[:, :nh], gs)
            q, kf, kb, vf, vb, lf, mul_p = proj["p"]
            lf3 = lf.reshape(bsz, seq, nh)
            op = _fox_attn_prompt(q, kb, vb, _cumsum_rows(lf3) * LOG2E, bsz, seq, hd)
            b_out["kp"].append(kf.reshape(bsz, seq, nh, hd))
            b_out["vp"].append(vf.reshape(bsz, seq, nh, hd))
            b_out["lp"].append(lf3)
            q, kf, kb, vf, vb, lf, mul_s = proj["s"]
            past = cache_b_k.shape[2]
            lf3 = lf.reshape(dbsz, dseq, nh)
            lf_all = jnp.concatenate([cache_b_logf[j].astype(F32), lf3], axis=1)
            cum = _cumsum_rows(lf_all).transpose(0, 2, 1) * LOG2E
            os_ = _fox_attn_sample(q, kb, vb, cache_b_k, cache_b_v, j,
                                   cum[:, :, past:, None], cum[:, :, None, :past],
                                   cum[:, :, None, past:], dseq, hd)
            b_out["ks"].append(kf.reshape(dbsz, dseq, nh, hd))
            b_out["vs"].append(vf.reshape(dbsz, dseq, nh, hd))
            b_out["ls"].append(lf3)
            w_out = b_w_out
        xp, = _mm(op, w_out, j, 0, d, epi="resid", mul=mul_p, xres=xp, gate=mod_p[2],
                  rows_per_seq=seq, out_dtypes=(F32,), name="out_proj")
        xs, = _mm(os_, w_out, j, 0, d, epi="resid", mul=mul_s, xres=xs, gate=mod_s[2],
                  rows_per_seq=ms, out_dtypes=(F32,), name="out_proj")

        cnt0 = jnp.zeros((ne, LANE), F32)
        h2p, h3p, eip, ewp, rkp, cnt1 = _router(xp, n2g, i, mod_p[3], mod_p[4], seq, router_wt, router_b3, cnt0)
        h2s, h3s, eis, ews, rks, cnt2 = _router(xs, n2g, i, mod_s[3], mod_s[4], ms, router_wt, router_b3, cnt1)
        h3 = jnp.concatenate([h3p, h3s], axis=0)
        ei = jnp.concatenate([eip, eis], axis=1)
        rank = jnp.concatenate([rkp, rks], axis=1)
        slot_token, pos, tile_expert, n_valid = _moe_plan(ei, rank, cnt2[:, 0].astype(jnp.int32), MOE_TILE)
        ys = _experts(h3, slot_token, tile_expert, n_valid, exp_w_gate, exp_w_up, exp_w_down, i)
        sp, = _mm(h2p, sh_gu, i, 0, sh_gu.shape[-1], epi="swiglu", out_dtypes=(BF16,), name="shared_up")
        ss, = _mm(h2s, sh_gu, i, 0, sh_gu.shape[-1], epi="swiglu", out_dtypes=(BF16,), name="shared_up")
        xp = _combine(xp, ys, pos[:, :mp], ewp.T, sp, sh_w_down, i, mod_p[5], seq)
        xs = _combine(xs, ys, pos[:, mp:], ews.T, ss, sh_w_down, i, mod_s[5], ms)

    st = jnp.stack
    return (xp.reshape(bsz, seq, d), xs.reshape(dbsz, dseq, d),
            st(a_out["kp"]), st(a_out["vp"]), st(a_out["ks"]), st(a_out["vs"]),
            st(b_out["kp"]), st(b_out["vp"]), st(b_out["lp"]),
            st(b_out["ks"]), st(b_out["vs"]), st(b_out["ls"]))
```

```python
import functools

import jax
import jax.numpy as jnp
import numpy as np
from jax import lax
from jax.experimental import pallas as pl
from jax.experimental.pallas import tpu as pltpu

F32 = jnp.float32
BF16 = jnp.bfloat16

CHUNK = 64
N_PREV_CHUNKS = 8
TOP_K = 8
N_GROUPS = 8
TOPK_GROUPS = 4
ROUTED_SCALE = 2.5
NORM_EPS = 1e-6
NEG_INF = -1e30
REMOVED = -3e38
LANE = 128
VMEM_LIMIT = 56 * 1024 * 1024
Q_TILE = 128
BAND_KEYS = (N_PREV_CHUNKS + 2) * CHUNK
FOX_TILE = 256
MOE_TILE = 256
LOG2E = 1.4426950408889634
CHUNK_HEADS = 4
FOX_HEADS = 4
EXPERT_CHUNKS = 4


def _pick(n, cands):
    for c in cands:
        if n % c == 0:
            return c
    return n


def _params(sem):
    return pltpu.CompilerParams(dimension_semantics=sem, vmem_limit_bytes=VMEM_LIMIT)


def _nt(a, b):
    return lax.dot_general(a, b, (((1,), (1,)), ((), ())), preferred_element_type=F32)


def _dot(a, b):
    return jnp.dot(a, b, preferred_element_type=F32)


def _mod_spec(mod, tm, tn, rows_per_seq, row_arg, col_arg):
    if mod.ndim == 3:
        def imap(*g):
            col = 0 if col_arg is None else g[col_arg]
            return ((g[row_arg] * tm) // rows_per_seq, 0, col)
        return pl.BlockSpec((None, 1, tn), imap)

    def imap2(*g):
        col = 0 if col_arg is None else g[col_arg]
        return (g[row_arg], col)
    return pl.BlockSpec((tm, tn), imap2)


def _adaln_kernel(c_ref, w_ref, b_ref, o_ref):
    c = c_ref[...]
    a = (c * jax.nn.sigmoid(c)).astype(BF16)
    o_ref[...] = _dot(a, w_ref[...].astype(BF16)) + b_ref[...]


def _adaln(c_all, ada_w, ada_b3, layer):
    r, d = c_all.shape
    n = ada_w.shape[2]
    tn = _pick(n, (1024, 512, 256, 128))
    return pl.pallas_call(
        _adaln_kernel,
        grid=(n // tn,),
        in_specs=[pl.BlockSpec((r, d), lambda j: (0, 0)),
                  pl.BlockSpec((None, d, tn), lambda j: (layer, 0, j)),
                  pl.BlockSpec((None, 1, tn), lambda j: (layer, 0, j))],
        out_specs=pl.BlockSpec((r, tn), lambda j: (0, j)),
        out_shape=jax.ShapeDtypeStruct((r, n), F32),
        compiler_params=_params(("arbitrary",)),
        name="adaln",
    )(c_all, ada_w, ada_b3)


def _modulated(x, g, sh, sc):
    ms = jnp.mean(x * x, axis=-1, keepdims=True)
    y = x * lax.rsqrt(ms + NORM_EPS) * g
    return y * (1.0 + sc) + sh


def _modulate_kernel(x_ref, g_ref, sh_ref, sc_ref, o_ref):
    o_ref[...] = _modulated(x_ref[...], g_ref[...], sh_ref[...], sc_ref[...]).astype(BF16)


def _modulate(x, g3, layer, shift, scale, rows_per_seq):
    m, d = x.shape
    tm = _pick(min(m, rows_per_seq), (512, 256, 128, 64, 32, 16, 8))
    return pl.pallas_call(
        _modulate_kernel,
        grid=(m // tm,),
        in_specs=[pl.BlockSpec((tm, d), lambda i: (i, 0)),
                  pl.BlockSpec((None, 1, d), lambda i: (layer, 0, 0)),
                  _mod_spec(shift, tm, d, rows_per_seq, 0, None),
                  _mod_spec(scale, tm, d, rows_per_seq, 0, None)],
        out_specs=pl.BlockSpec((tm, d), lambda i: (i, 0)),
        out_shape=jax.ShapeDtypeStruct((m, d), BF16),
        compiler_params=_params(("arbitrary",)),
        name="modulate",
    )(x, g3, shift, scale)


def _mm_kernel(*refs, epi, has_mul, n_out, scale):
    it = iter(refs)
    a_ref = next(it)
    m_ref = next(it) if has_mul else None
    w_ref = next(it)
    extra = []
    n_extra = {"headnorm": 1, "plain": 0, "logsig": 1, "sigmoid": 0, "resid": 2, "swiglu": 0}[epi]
    for _ in range(n_extra):
        extra.append(next(it))
    outs = [next(it) for _ in range(n_out)]
    wb_ref = next(it)

    @pl.when(pl.program_id(1) == 0)
    def _():
        wb_ref[...] = w_ref[...].astype(BF16)

    a = a_ref[...]
    if has_mul:
        a = a * m_ref[...]
    acc = _dot(a, wb_ref[...])

    if epi == "headnorm":
        g = extra[0][...]
        hd = g.shape[-1]
        ys = []
        for h in range(acc.shape[1] // hd):
            blk = acc[:, h * hd:(h + 1) * hd]
            ms = jnp.mean(blk * blk, axis=-1, keepdims=True)
            ys.append(blk * lax.rsqrt(ms + NORM_EPS) * g)
        res = jnp.concatenate(ys, axis=1)
    elif epi == "plain":
        res = acc
    elif epi == "logsig":
        z = acc + extra[0][...]
        res = jnp.minimum(z, 0.0) - jnp.log1p(jnp.exp(-jnp.abs(z)))
    elif epi == "sigmoid":
        res = jax.nn.sigmoid(acc)
    elif epi == "resid":
        res = extra[0][...] + extra[1][...] * acc
    elif epi == "swiglu":
        f = acc.shape[1] // 2
        ga = acc[:, :f]
        res = ga * jax.nn.sigmoid(ga) * acc[:, f:]
    for o in outs:
        val = res * scale if (o.dtype == BF16 and scale != 1.0) else res
        if len(o.shape) == 3:
            val = val.reshape(o.shape)
        o[...] = val.astype(o.dtype)


def _mm(a, w, layer, col0, n, *, epi, out_dtypes, mul=None, g=None, bias=None, xres=None,
        gate=None, rows_per_seq=None, scale=1.0, head_dim=None, name="mm"):
    m, k = a.shape
    if epi == "swiglu":
        tn = n
    else:
        tn = _pick(n, (1024, 512, 256, 128))
    assert col0 % tn == 0
    cb = col0 // tn
    tm = _pick(m, (512, 256, 128))
    if rows_per_seq is not None:
        assert rows_per_seq % tm == 0 or (gate is not None and gate.ndim == 2)
    n_out_cols = n // 2 if epi == "swiglu" else n
    tno = tn // 2 if epi == "swiglu" else tn

    args = [a]
    specs = [pl.BlockSpec((tm, k), lambda j, i: (i, 0))]
    if mul is not None:
        args.append(mul)
        specs.append(pl.BlockSpec((tm, k), lambda j, i: (i, 0)))
    args.append(w)
    specs.append(pl.BlockSpec((None, k, tn), lambda j, i: (layer, 0, cb + j)))
    if epi == "headnorm":
        args.append(g)
        specs.append(pl.BlockSpec((None, 1, g.shape[-1]), lambda j, i: (layer, 0, 0)))
    elif epi == "logsig":
        args.append(bias)
        specs.append(pl.BlockSpec((None, 1, tn), lambda j, i: (layer, 0, j)))
    elif epi == "resid":
        args.append(xres)
        specs.append(pl.BlockSpec((tm, tn), lambda j, i: (i, j)))
        args.append(gate)
        specs.append(_mod_spec(gate, tm, tn, rows_per_seq, 1, 0))
    out_shape, out_specs = [], []
    for dt in out_dtypes:
        if head_dim is not None and dt == F32:
            out_shape.append(jax.ShapeDtypeStruct((m, n_out_cols // head_dim, head_dim), dt))
            out_specs.append(pl.BlockSpec((tm, tno // head_dim, head_dim), lambda j, i: (i, j, 0)))
        else:
            out_shape.append(jax.ShapeDtypeStruct((m, n_out_cols), dt))
            out_specs.append(pl.BlockSpec((tm, tno), lambda j, i: (i, j)))
    res = pl.pallas_call(
        functools.partial(_mm_kernel, epi=epi, has_mul=mul is not None, n_out=len(out_dtypes),
                          scale=scale),
        grid=(n // tn, m // tm),
        in_specs=specs,
        out_specs=out_specs,
        out_shape=out_shape,
        scratch_shapes=[pltpu.VMEM((k, tn), BF16)],
        compiler_params=_params(("arbitrary", "arbitrary")),
        name=name,
    )(*args)
    return res


def _chunk_prompt_kernel(q_ref, k_ref, v_ref, b_ref, o_ref, *, n_tiles, edge, hd):
    heads = q_ref.shape[1] // hd

    def body(m, carry):
        q0 = pl.multiple_of(m * Q_TILE, Q_TILE)
        k0 = pl.multiple_of(jnp.maximum(m - edge, 0) * Q_TILE, Q_TILE)
        bi = jnp.minimum(m, edge)
        sls = [slice(g * hd, (g + 1) * hd) for g in range(heads)]
        ss = [_nt(q_ref[pl.ds(q0, Q_TILE), sl], k_ref[pl.ds(k0, BAND_KEYS), sl]) for sl in sls]
        ps, ls = [], []
        for g in range(heads):
            s = ss[g] + b_ref[g, bi]
            mx = jnp.max(s, axis=-1, keepdims=True)
            p = jnp.exp2(s - mx)
            ls.append(jnp.sum(p, axis=-1, keepdims=True))
            ps.append(p.astype(BF16))
        for g in range(heads):
            o = _dot(ps[g], v_ref[pl.ds(k0, BAND_KEYS), sls[g]]) / ls[g]
            o_ref[pl.ds(q0, Q_TILE), sls[g]] = o.astype(BF16)
        return carry
    lax.fori_loop(0, n_tiles, body, 0)


def _chunk_bias_prompt(table):
    nh = table.shape[0]
    rel_max = (table.shape[1] - 1) // 2
    edge = (BAND_KEYS - Q_TILE) // Q_TILE
    mm = np.arange(edge + 1)[:, None, None]
    qpos = mm * Q_TILE + np.arange(Q_TILE)[None, :, None]
    kpos = np.maximum(mm - edge, 0) * Q_TILE + np.arange(BAND_KEYS)[None, None, :]
    qc, kc = qpos // CHUNK, kpos // CHUNK
    vis = (kc <= qc) & (kc >= qc - N_PREV_CHUNKS)
    period = Q_TILE + BAND_KEYS - 1
    c_m = np.minimum(np.arange(edge + 1), edge) * Q_TILE
    n = np.arange(period)
    n = np.where(n < BAND_KEYS, n, n - period)[None, :]
    idx = np.clip(c_m[:, None] - n, -rel_max, rel_max) + rel_max
    w = table[:, idx].astype(F32) * LOG2E
    skew = jnp.tile(w, (1, 1, Q_TILE))[..., :Q_TILE * (period - 1)]
    b = skew.reshape(nh, edge + 1, Q_TILE, period - 1)[..., :BAND_KEYS]
    return jnp.where(vis[None], b, NEG_INF), edge


def _chunk_attn_prompt(q, k, v, bias, edge, bsz, seq, hd):
    d = q.shape[1]
    nh = d // hd
    heads = _pick(nh, (CHUNK_HEADS, 2, 1))
    q3, k3, v3 = (t.reshape(bsz, seq, d) for t in (q, k, v))
    spec = pl.BlockSpec((None, seq, heads * hd), lambda b, h: (b, 0, h))
    o = pl.pallas_call(
        functools.partial(_chunk_prompt_kernel, n_tiles=seq // Q_TILE, edge=edge, hd=hd),
        grid=(bsz, nh // heads),
        in_specs=[spec, spec, spec,
                  pl.BlockSpec((heads, edge + 1, Q_TILE, BAND_KEYS), lambda b, h: (h, 0, 0, 0))],
        out_specs=spec,
        out_shape=jax.ShapeDtypeStruct((bsz, seq, d), BF16),
        compiler_params=_params(("arbitrary", "arbitrary")),
        name="chunk_attn_prompt",
    )(q3, k3, v3, bias)
    return o.reshape(bsz * seq, d)


def _sample_attn_kernel(*refs, hd, fox):
    if fox:
        q_ref, kn_ref, vn_ref, kc_ref, vc_ref, cq_ref, ckc_ref, ckn_ref, o_ref, ks_scr, vs_scr = refs
    else:
        q_ref, kn_ref, vn_ref, kc_ref, vc_ref, bc_ref, bn_ref, o_ref, ks_scr, vs_scr = refs
    t = q_ref.shape[0]
    nh = q_ref.shape[1] // hd
    past = kc_ref.shape[0]
    rows = _pick(past, (512, 256, 128))
    for c in range(past // rows):
        rs = slice(c * rows, (c + 1) * rows)
        ks_scr[rs, :] = kc_ref[rs].reshape(rows, nh * hd).astype(BF16)
        vs_scr[rs, :] = vc_ref[rs].reshape(rows, nh * hd).astype(BF16)
    if fox:
        row = lax.broadcasted_iota(jnp.int32, (t, t), 0)
        col = lax.broadcasted_iota(jnp.int32, (t, t), 1)
        causal = col <= row
    for h in range(nh):
        sl = slice(h * hd, (h + 1) * hd)
        q = q_ref[:, sl]
        kc = ks_scr[:, sl]
        vc = vs_scr[:, sl]
        s1 = _nt(q, kc)
        s2 = _nt(q, kn_ref[:, sl])
        if fox:
            cq = cq_ref[h]
            s1 = s1 + cq - ckc_ref[h]
            s2 = jnp.where(causal, s2 + cq - ckn_ref[h], NEG_INF)
        else:
            s1 = s1 + bc_ref[h]
            s2 = s2 + bn_ref[h]
        mx = jnp.maximum(jnp.max(s1, axis=-1, keepdims=True), jnp.max(s2, axis=-1, keepdims=True))
        p1 = jnp.exp2(s1 - mx)
        p2 = jnp.exp2(s2 - mx)
        l = jnp.sum(p1, axis=-1, keepdims=True) + jnp.sum(p2, axis=-1, keepdims=True)
        o = (_dot(p1.astype(BF16), vc) + _dot(p2.astype(BF16), vn_ref[:, sl])) / l
        o_ref[:, sl] = o.astype(BF16)


def _chunk_attn_sample(q, kn, vn, cache_k, cache_v, layer, bias_c, bias_n, t, hd):
    _, bsz, w, nh, _ = cache_k.shape
    d = nh * hd
    hg = _pick(nh, (8,)) * hd
    ng = d // hg
    tok = pl.BlockSpec((t, hg), lambda b, g: (b, g))
    cache = pl.BlockSpec((None, None, w, hg // hd, hd), lambda b, g: (layer, b, 0, g, 0))
    return pl.pallas_call(
        functools.partial(_sample_attn_kernel, hd=hd, fox=False),
        grid=(bsz, ng),
        in_specs=[tok, tok, tok, cache, cache,
                  pl.BlockSpec((hg // hd, t, w), lambda b, g: (g, 0, 0)),
                  pl.BlockSpec((hg // hd, t, t), lambda b, g: (g, 0, 0))],
        out_specs=tok,
        out_shape=jax.ShapeDtypeStruct((bsz * t, d), BF16),
        scratch_shapes=[pltpu.VMEM((w, hg), BF16), pltpu.VMEM((w, hg), BF16)],
        compiler_params=_params(("arbitrary", "arbitrary")),
        name="chunk_attn_sample",
    )(q, kn, vn, cache_k, cache_v, bias_c, bias_n)


def _fox_attn_sample(q, kn, vn, cache_k, cache_v, layer, cq, ckc, ckn, t, hd):
    _, bsz, w, nh, _ = cache_k.shape
    d = nh * hd
    hg = _pick(nh, (8,)) * hd
    ng = d // hg
    gh = hg // hd
    tok = pl.BlockSpec((t, hg), lambda b, g: (b, g))
    cache = pl.BlockSpec((None, None, w, gh, hd), lambda b, g: (layer, b, 0, g, 0))
    return pl.pallas_call(
        functools.partial(_sample_attn_kernel, hd=hd, fox=True),
        grid=(bsz, ng),
        in_specs=[tok, tok, tok, cache, cache,
                  pl.BlockSpec((None, gh, t, 1), lambda b, g: (b, g, 0, 0)),
                  pl.BlockSpec((None, gh, 1, w), lambda b, g: (b, g, 0, 0)),
                  pl.BlockSpec((None, gh, 1, t), lambda b, g: (b, g, 0, 0))],
        out_specs=tok,
        out_shape=jax.ShapeDtypeStruct((bsz * t, d), BF16),
        scratch_shapes=[pltpu.VMEM((w, hg), BF16), pltpu.VMEM((w, hg), BF16)],
        compiler_params=_params(("arbitrary", "arbitrary")),
        name="fox_attn_sample",
    )(q, kn, vn, cache_k, cache_v, cq, ckc, ckn)


def _fox_prompt_kernel(q_ref, k_ref, v_ref, cq_ref, ck_ref, o_ref, *scr):
    i = pl.program_id(2)
    tq = q_ref.shape[0]
    heads = cq_ref.shape[0]
    hd = q_ref.shape[1] // heads
    m_scr, l_scr, acc_scr = scr[:heads], scr[heads:2 * heads], scr[2 * heads:3 * heads]
    qt_scr = scr[3 * heads:]
    row = lax.broadcasted_iota(jnp.int32, (tq, tq), 0)
    col = lax.broadcasted_iota(jnp.int32, (tq, tq), 1)
    for g in range(heads):
        m_scr[g][...] = jnp.full((1, tq), NEG_INF, F32)
        l_scr[g][...] = jnp.zeros((1, tq), F32)
        acc_scr[g][...] = jnp.zeros(acc_scr[g].shape, F32)
        qt_scr[g][...] = q_ref[:, g * hd:(g + 1) * hd].astype(F32).T.astype(BF16)

    def step(j, diag):
        k0 = pl.multiple_of(j * tq, tq)
        sts = [_dot(k_ref[pl.ds(k0, tq), g * hd:(g + 1) * hd], qt_scr[g][...]) for g in range(heads)]
        pts, alphas = [], []
        for g in range(heads):
            ck = ck_ref[g, pl.ds(k0, tq), :]
            st = sts[g] + cq_ref[g] - jnp.concatenate([ck] * (tq // LANE), axis=1)
            if diag:
                st = jnp.where(row <= col, st, NEG_INF)
            m_old = m_scr[g][...]
            m_new = jnp.maximum(m_old, jnp.max(st, axis=0, keepdims=True))
            alpha = jnp.exp2(m_old - m_new)
            pt = jnp.exp2(st - m_new)
            l_scr[g][...] = alpha * l_scr[g][...] + jnp.sum(pt, axis=0, keepdims=True)
            m_scr[g][...] = m_new
            pts.append(pt.astype(BF16))
            alphas.append(alpha)
        for g in range(heads):
            vb = v_ref[pl.ds(k0, tq), g * hd:(g + 1) * hd]
            pv = lax.dot_general(vb, pts[g], (((0,), (0,)), ((), ())), preferred_element_type=F32)
            acc_scr[g][...] = alphas[g] * acc_scr[g][...] + pv

    def body(j, carry):
        step(j, False)
        return carry
    lax.fori_loop(0, i, body, 0)
    step(i, True)
    for g in range(heads):
        o_ref[:, g * hd:(g + 1) * hd] = (acc_scr[g][...] / l_scr[g][...]).T.astype(BF16)


def _cumsum_rows(x):
    b, s, h = x.shape
    pad = -s % LANE
    xb = jnp.pad(x, ((0, 0), (0, pad), (0, 0))).reshape(b, (s + pad) // LANE, LANE, h)
    tri = jnp.tril(jnp.ones((LANE, LANE), F32))
    within = jnp.einsum("ij,bnjh->bnih", tri, xb, precision=lax.Precision.HIGHEST)
    tot = within[:, :, -1, :]
    offs = jnp.cumsum(tot, axis=1) - tot
    return (within + offs[:, :, None, :]).reshape(b, s + pad, h)[:, :s]


def _fox_attn_prompt(q, k, v, cum, bsz, seq, hd):
    d = q.shape[1]
    nh = d // hd
    heads = _pick(nh, (FOX_HEADS, 2, 1))
    tq = _pick(seq, (FOX_TILE, 128))
    cum_t = cum.transpose(0, 2, 1)
    cq = cum_t[:, :, None, :]
    ck = jnp.broadcast_to(cum_t[..., None], (bsz, nh, seq, LANE))
    q3, k3, v3 = (t.reshape(bsz, seq, d) for t in (q, k, v))
    qspec = pl.BlockSpec((None, tq, heads * hd), lambda b, h, i: (b, i, h))
    kspec = pl.BlockSpec((None, seq, heads * hd), lambda b, h, i: (b, 0, h))
    o = pl.pallas_call(
        _fox_prompt_kernel,
        grid=(bsz, nh // heads, seq // tq),
        in_specs=[qspec, kspec, kspec,
                  pl.BlockSpec((None, heads, 1, tq), lambda b, h, i: (b, h, 0, i)),
                  pl.BlockSpec((None, heads, seq, LANE), lambda b, h, i: (b, h, 0, 0))],
        out_specs=qspec,
        out_shape=jax.ShapeDtypeStruct((bsz, seq, d), BF16),
        scratch_shapes=([pltpu.VMEM((1, tq), F32)] * (2 * heads)
                        + [pltpu.VMEM((hd, tq), F32)] * heads
                        + [pltpu.VMEM((hd, tq), BF16)] * heads),
        compiler_params=_params(("arbitrary", "arbitrary", "arbitrary")),
        name="fox_attn_prompt",
    )(q3, k3, v3, cq, ck)
    return o.reshape(bsz * seq, d)


def _router_kernel(x_ref, g_ref, sh_ref, sc_ref, wt_ref, rb_ref, cin_ref,
                   h_ref, h3_ref, ei_ref, ew_ref, rk_ref, cout_ref, cnt):
    @pl.when(pl.program_id(0) == 0)
    def _():
        cnt[...] = cin_ref[...]

    hf = _modulated(x_ref[...], g_ref[...], sh_ref[...], sc_ref[...])
    hh = hf.astype(BF16)
    h_ref[...] = hh
    h3_ref[...] = hf.reshape(h3_ref.shape).astype(BF16)
    hl = (hf - hh.astype(F32)).astype(BF16)
    w = wt_ref[...]
    wh = w.astype(BF16)
    wl = (w - wh.astype(F32)).astype(BF16)
    logits = _nt(wh, hh) + (_nt(wh, hl) + _nt(wl, hh))
    scores = jax.nn.sigmoid(logits)
    sel = scores + rb_ref[...]
    ne, tm = sel.shape
    gsz = ne // N_GROUPS
    row_g = lax.broadcasted_iota(jnp.int32, (gsz, tm), 0).astype(F32)
    row_n = lax.broadcasted_iota(jnp.int32, (N_GROUPS, tm), 0).astype(F32)
    row_e = lax.broadcasted_iota(jnp.int32, (ne, tm), 0).astype(F32)

    def first_max(x, rows, big):
        mx = jnp.max(x, axis=0, keepdims=True)
        ix = jnp.min(jnp.where(x == mx, rows, big), axis=0, keepdims=True)
        return mx, ix

    grp = jnp.zeros((N_GROUPS, tm), F32)
    for g in range(N_GROUPS):
        blk = sel[g * gsz:(g + 1) * gsz, :]
        m1, i1 = first_max(blk, row_g, float(gsz))
        m2 = jnp.max(jnp.where(row_g == i1, REMOVED, blk), axis=0, keepdims=True)
        grp = jnp.where(row_n == float(g), m1 + m2, grp)
    gsel = jnp.zeros((N_GROUPS, tm), F32)
    work = grp
    for _ in range(TOPK_GROUPS):
        _, ix = first_max(work, row_n, float(N_GROUPS))
        hit = row_n == ix
        gsel = jnp.where(hit, 1.0, gsel)
        work = jnp.where(hit, REMOVED, work)
    work = jnp.concatenate(
        [jnp.where(gsel[g:g + 1, :] > 0.0, sel[g * gsz:(g + 1) * gsz, :], NEG_INF)
         for g in range(N_GROUPS)], axis=0)
    row_k = lax.broadcasted_iota(jnp.int32, (TOP_K, tm), 0)
    ei = jnp.zeros((TOP_K, tm), F32)
    ew = jnp.zeros((TOP_K, tm), F32)
    hits = []
    chosen = jnp.zeros((ne, tm), F32)
    for r in range(TOP_K):
        _, ix = first_max(work, row_e, float(ne))
        hit = row_e == ix
        hits.append(hit)
        chosen = jnp.where(hit, 1.0, chosen)
        wr = jnp.sum(jnp.where(hit, scores, 0.0), axis=0, keepdims=True)
        ei = jnp.where(row_k == r, ix, ei)
        ew = jnp.where(row_k == r, wr, ew)
        work = jnp.where(hit, REMOVED, work)
    ew = ew / jnp.sum(ew, axis=0, keepdims=True) * ROUTED_SCALE
    ei_ref[...] = ei.astype(jnp.int32)
    ew_ref[...] = ew
    cb = chosen.astype(BF16)
    tr = lax.broadcasted_iota(jnp.int32, (tm, tm), 0)
    tc = lax.broadcasted_iota(jnp.int32, (tm, tm), 1)
    before = _dot(cb, jnp.where(tr < tc, 1.0, 0.0).astype(BF16))
    total = _dot(cb, jnp.ones((tm, LANE), BF16))
    base = cnt[...]
    rank_e = jnp.concatenate([base] * (tm // LANE), axis=1) + before
    rk = jnp.zeros((TOP_K, tm), F32)
    for r in range(TOP_K):
        rr = jnp.sum(jnp.where(hits[r], rank_e, 0.0), axis=0, keepdims=True)
        rk = jnp.where(row_k == r, rr, rk)
    rk_ref[...] = rk.astype(jnp.int32)
    cnt[...] = base + total
    cout_ref[...] = base + total


def _router(x, g3, layer, shift, scale, rows_per_seq, router_wt, router_b3, cnt_in):
    m, d = x.shape
    ne = router_wt.shape[1]
    tm = _pick(m, (256, 128))
    asg = pl.BlockSpec((TOP_K, tm), lambda i: (0, i))
    return pl.pallas_call(
        _router_kernel,
        grid=(m // tm,),
        in_specs=[pl.BlockSpec((tm, d), lambda i: (i, 0)),
                  pl.BlockSpec((None, 1, d), lambda i: (layer, 0, 0)),
                  _mod_spec(shift, tm, d, rows_per_seq, 0, None),
                  _mod_spec(scale, tm, d, rows_per_seq, 0, None),
                  pl.BlockSpec((None, ne, d), lambda i: (layer, 0, 0)),
                  pl.BlockSpec((None, ne, 1), lambda i: (layer, 0, 0)),
                  pl.BlockSpec((ne, LANE), lambda i: (0, 0))],
        out_specs=[pl.BlockSpec((tm, d), lambda i: (i, 0)),
                   pl.BlockSpec((tm, d // LANE, LANE), lambda i: (i, 0, 0)), asg, asg, asg,
                   pl.BlockSpec((ne, LANE), lambda i: (0, 0))],
        out_shape=[jax.ShapeDtypeStruct((m, d), BF16),
                   jax.ShapeDtypeStruct((m, d // LANE, LANE), BF16),
                   jax.ShapeDtypeStruct((TOP_K, m), jnp.int32),
                   jax.ShapeDtypeStruct((TOP_K, m), F32),
                   jax.ShapeDtypeStruct((TOP_K, m), jnp.int32),
                   jax.ShapeDtypeStruct((ne, LANE), F32)],
        scratch_shapes=[pltpu.VMEM((ne, LANE), F32)],
        compiler_params=_params(("arbitrary",)),
        name="router",
    )(x, g3, shift, scale, router_wt, router_b3, cnt_in)


def _experts_kernel(te_ref, nv_ref, tok_ref, nxt_ref, h_ref, wg_ref, wu_ref, wd_ref, y_ref,
                    wgu, wdb, xs_scr, xbuf, sem):
    t = pl.program_id(0)
    nv = nv_ref[0]
    tm = xbuf.shape[1]
    d = wg_ref.shape[0]
    slot = lax.rem(t, 2)
    e = te_ref[t]
    prev = te_ref[jnp.maximum(t - 1, 0)]

    def row_copy(idx_ref, r, s):
        return pltpu.make_async_copy(h_ref.at[idx_ref[0, r]], xbuf.at[s, r], sem.at[s])

    @pl.when(t == 0)
    def _():
        for r in range(tm):
            row_copy(tok_ref, r, 0).start(priority=r % 2)

    f = wg_ref.shape[1]

    @pl.when((t == 0) | (e != prev))
    def _():
        wgu[:, :f] = wg_ref[...].astype(BF16)
        wgu[:, f:] = wu_ref[...].astype(BF16)
        wdb[...] = wd_ref[...].astype(BF16)

    @pl.when(t < nv)
    def _():
        for r in range(tm):
            row_copy(tok_ref, r, slot).wait()
        xs_scr[...] = xbuf[slot].astype(F32).reshape(tm, d).astype(BF16)
        kc, rc = d // EXPERT_CHUNKS, tm // EXPERT_CHUNKS

        def chunk(c, acc):
            for i in range(rc):
                row_copy(nxt_ref, c * rc + i, 1 - slot).start(priority=i % 2)
            k0 = pl.multiple_of(c * kc, kc)
            return acc + _dot(xs_scr[:, pl.ds(k0, kc)], wgu[pl.ds(k0, kc), :])
        acc = lax.fori_loop(0, EXPERT_CHUNKS, chunk, jnp.zeros((tm, 2 * f), F32))
        a, b = acc[:, :f], acc[:, f:]
        s = (a * jax.nn.sigmoid(a) * b).astype(BF16)
        y_ref[...] = _dot(s, wdb[...]).reshape(y_ref.shape).astype(BF16)

    @pl.when(t == nv - 1)
    def _():
        for r in range(tm):
            row_copy(nxt_ref, r, 1 - slot).wait()

    @pl.when(t >= nv)
    def _():
        y_ref[...] = jnp.zeros(y_ref.shape, BF16)


def _experts(h3, slot_token, tile_expert, n_valid, wg, wu, wd, layer):
    _, nl, lane = h3.shape
    d = nl * lane
    f = wg.shape[-1]
    tm = MOE_TILE
    n_tiles = slot_token.shape[0] // tm
    tok3 = slot_token.reshape(n_tiles, 1, tm)

    def tok_spec(shift):
        return pl.BlockSpec((None, 1, tm), lambda t, te, nv: (jnp.minimum(t + shift, nv[0] - 1), 0, 0),
                            memory_space=pltpu.SMEM)
    grid_spec = pltpu.PrefetchScalarGridSpec(
        num_scalar_prefetch=2,
        grid=(n_tiles,),
        in_specs=[tok_spec(0), tok_spec(1),
                  pl.BlockSpec(memory_space=pl.ANY),
                  pl.BlockSpec((None, None, d, f), lambda t, te, nv: (layer, te[t], 0, 0)),
                  pl.BlockSpec((None, None, d, f), lambda t, te, nv: (layer, te[t], 0, 0)),
                  pl.BlockSpec((None, None, f, d), lambda t, te, nv: (layer, te[t], 0, 0))],
        out_specs=pl.BlockSpec((tm, nl, lane), lambda t, te, nv: (t, 0, 0)),
        scratch_shapes=[pltpu.VMEM((d, 2 * f), BF16), pltpu.VMEM((f, d), BF16), pltpu.VMEM((tm, d), BF16),
                        pltpu.VMEM((2, tm, nl, lane), BF16), pltpu.SemaphoreType.DMA((2,))],
    )
    return pl.pallas_call(
        _experts_kernel,
        grid_spec=grid_spec,
        out_shape=jax.ShapeDtypeStruct((n_tiles * tm, nl, lane), BF16),
        compiler_params=_params(("arbitrary",)),
        name="experts",
    )(tile_expert, n_valid, tok3, tok3, h3, wg, wu, wd)


def _moe_plan(ei, rank, counts, tm):
    kk, t = ei.shape
    ne = counts.shape[0]
    n_asg = kk * t
    n_tiles = n_asg // tm + ne
    i32 = jnp.int32
    ar = jnp.arange(ne, dtype=i32)
    tiles_per = (counts + tm - 1) // tm
    tile_end = jnp.cumsum(tiles_per)
    pstart = (tile_end - tiles_per) * tm
    ustart = jnp.cumsum(counts) - counts
    pos = rank + jnp.sum(jnp.where(ei[..., None] == ar, pstart, 0), axis=-1)
    n_valid = tile_end[-1]
    tile_ids = jnp.arange(n_tiles, dtype=i32)
    te = jnp.sum((tile_end[None, :] <= tile_ids[:, None]).astype(i32), axis=1)
    te = jnp.minimum(te, jnp.sum((tile_end <= n_valid - 1).astype(i32)))
    onehot = te[:, None] == ar[None, :]
    t_pstart, t_count, t_ustart = (jnp.sum(jnp.where(onehot, v, 0), axis=1)
                                   for v in (pstart, counts, ustart))
    within = tile_ids[:, None] * tm + jnp.arange(tm, dtype=i32)[None, :] - t_pstart[:, None]
    valid = (within < t_count[:, None]) & (tile_ids[:, None] < n_valid)
    order = jnp.argsort(ei.T.reshape(-1), stable=True)
    src = jnp.take(order, jnp.clip(t_ustart[:, None] + within, 0, n_asg - 1).reshape(-1)) // kk
    slot_token = jnp.where(valid.reshape(-1), src, 0).astype(i32)
    return slot_token, pos.astype(i32), te.astype(i32), n_valid.reshape(1).astype(i32)


def _combine_kernel(pos_ref, x_ref, ys_ref, w_ref, sa_ref, sd_ref, gate_ref, o_ref, sdb, ybuf, sem):
    @pl.when(pl.program_id(0) == 0)
    def _():
        sdb[...] = sd_ref[...].astype(BF16)

    tm, d = x_ref.shape

    def row_copy(k, r):
        return pltpu.make_async_copy(ys_ref.at[pos_ref[k, r]], ybuf.at[k * tm + r], sem)

    def issue(r, carry):
        for k in range(TOP_K):
            row_copy(k, r).start(priority=k % 2)
        return carry
    lax.fori_loop(0, tm, issue, 0)

    w = w_ref[...]
    acc = _dot(sa_ref[...], sdb[...])

    for r in range(tm):
        for k in range(TOP_K):
            row_copy(k, r).wait()

    for k in range(TOP_K):
        yk = ybuf[pl.ds(k * tm, tm)].astype(F32).reshape(tm, d)
        acc = acc + w[:, k:k + 1] * yk
    o_ref[...] = x_ref[...] + gate_ref[...] * acc


def _combine(x, ys, pos, w, sact, sh_w_down, layer, gate, rows_per_seq):
    m, d = x.shape
    f = sact.shape[1]
    tm = _pick(m, (256, 128))
    return pl.pallas_call(
        _combine_kernel,
        grid=(m // tm,),
        in_specs=[pl.BlockSpec((TOP_K, tm), lambda i: (0, i), memory_space=pltpu.SMEM),
                  pl.BlockSpec((tm, d), lambda i: (i, 0)),
                  pl.BlockSpec(memory_space=pl.ANY),
                  pl.BlockSpec((tm, TOP_K), lambda i: (i, 0)),
                  pl.BlockSpec((tm, f), lambda i: (i, 0)),
                  pl.BlockSpec((None, f, d), lambda i: (layer, 0, 0)),
                  _mod_spec(gate, tm, d, rows_per_seq, 0, None)],
        out_specs=pl.BlockSpec((tm, d), lambda i: (i, 0)),
        out_shape=jax.ShapeDtypeStruct((m, d), F32),
        scratch_shapes=[pltpu.VMEM((f, d), BF16),
                        pltpu.VMEM((TOP_K * tm, d // LANE, LANE), BF16),
                        pltpu.SemaphoreType.DMA],
        compiler_params=_params(("arbitrary",)),
        name="combine",
    )(pos, x, ys, w, sact, sh_w_down, gate)


def kernel(x_prompt, x_sample, c_prompt, c_sample, cache_a_k, cache_a_v, cache_b_k, cache_b_v,
           cache_b_logf, ada_w, ada_b, norm1_g, norm2_g, a_w_in, a_q_g, a_k_g, a_rel_table, a_w_out,
           b_w_in, b_f_bias, b_q_g, b_k_g, b_w_out, router_w, router_bias, exp_w_gate, exp_w_up,
           exp_w_down, sh_w_gate, sh_w_up, sh_w_down):
    bsz, seq, d = x_prompt.shape
    dbsz, dseq, _ = x_sample.shape
    depth = ada_w.shape[0]
    hd = a_q_g.shape[-1]
    nh = d // hd
    ne = router_w.shape[-1]
    mp, ms = bsz * seq, dbsz * dseq
    attn_scale = hd ** -0.5 * LOG2E

    xp = x_prompt.reshape(mp, d)
    xs = x_sample.reshape(ms, d)
    c_all = jnp.concatenate([c_prompt, c_sample], axis=0)
    ada_b3 = ada_b.reshape(depth, 1, -1)
    n1g = norm1_g.reshape(depth, 1, d)
    n2g = norm2_g.reshape(depth, 1, d)
    router_wt = jnp.swapaxes(router_w, 1, 2)
    router_b3 = router_bias.reshape(depth, ne, 1)
    sh_gu = jnp.concatenate([sh_w_gate, sh_w_up], axis=-1)

    a_out = {n: [] for n in ("kp", "vp", "ks", "vs")}
    b_out = {n: [] for n in ("kp", "vp", "lp", "ks", "vs", "ls")}

    for i in range(depth):
        mods = _adaln(c_all, ada_w, ada_b3, i)
        mod6 = jnp.split(mods, 6, axis=-1)
        mod_p = [t[:bsz].reshape(bsz, 1, d) for t in mod6]
        mod_s = [jnp.repeat(t[bsz:], dseq, axis=0) for t in mod6]
        hp = _modulate(xp, n1g, i, mod_p[0], mod_p[1], seq)
        hs = _modulate(xs, n1g, i, mod_s[0], mod_s[1], ms)
        j = i // 2
        if i % 2 == 0:
            qg, kg = a_q_g.reshape(-1, 1, hd), a_k_g.reshape(-1, 1, hd)
            proj = {}
            for nm, h_in in (("p", hp), ("s", hs)):
                q, = _mm(h_in, a_w_in, j, 0, d, epi="headnorm", g=qg, scale=attn_scale,
                         out_dtypes=(BF16,), name="a_q")
                kf, kb = _mm(h_in, a_w_in, j, d, d, epi="headnorm", g=kg,
                             out_dtypes=(F32, BF16), head_dim=hd, name="a_k")
                vf, vb = _mm(h_in, a_w_in, j, 2 * d, d, epi="plain",
                             out_dtypes=(F32, BF16), head_dim=hd, name="a_v")
                proj[nm] = (q, kf, kb, vf, vb)
            q, kf, kb, vf, vb = proj["p"]
            bias_p, edge = _chunk_bias_prompt(a_rel_table[j])
            op = _chunk_attn_prompt(q, kb, vb, bias_p, edge, bsz, seq, hd)
            keep = min(N_PREV_CHUNKS * CHUNK, seq)
            a_out["kp"].append(kf.reshape(bsz, seq, nh, hd)[:, seq - keep:])
            a_out["vp"].append(vf.reshape(bsz, seq, nh, hd)[:, seq - keep:])
            q, kf, kb, vf, vb = proj["s"]
            w = cache_a_k.shape[2]
            rel_max = (a_rel_table.shape[-1] - 1) // 2
            rel = np.arange(dseq)[:, None] + w - np.arange(w + dseq)[None, :]
            bias_s = a_rel_table[j][:, np.clip(rel, -rel_max, rel_max) + rel_max].astype(F32) * LOG2E
            os_ = _chunk_attn_sample(q, kb, vb, cache_a_k, cache_a_v, j,
                                     bias_s[:, :, :w], bias_s[:, :, w:], dseq, hd)
            a_out["ks"].append(kf.reshape(dbsz, dseq, nh, hd))
            a_out["vs"].append(vf.reshape(dbsz, dseq, nh, hd))
            mul_p = mul_s = None
            w_out = a_w_out
        else:
            qg, kg = b_q_g.reshape(-1, 1, hd), b_k_g.reshape(-1, 1, hd)
            w_f = jnp.pad(b_w_in[j][:, 3 * d:3 * d + nh], ((0, 0), (0, LANE - nh)))[None]
            f_b = jnp.pad(b_f_bias[j], (0, LANE - nh)).reshape(1, 1, LANE)
            w_g = b_w_in[j][:, 3 * d + nh:][None]
            proj = {}
            for nm, h_in in (("p", hp), ("s", hs)):
                q, = _mm(h_in, b_w_in, j, 0, d, epi="headnorm", g=qg, scale=attn_scale,
                         out_dtypes=(BF16,), name="b_q")
                kf, kb = _mm(h_in, b_w_in, j, d, d, epi="headnorm", g=kg,
                             out_dtypes=(F32, BF16), head_dim=hd, name="b_k")
                vf, vb = _mm(h_in, b_w_in, j, 2 * d, d, epi="plain",
                             out_dtypes=(F32, BF16), head_dim=hd, name="b_v")
                lf, = _mm(h_in, w_f, 0, 0, LANE, epi="logsig", bias=f_b, out_dtypes=(F32,),
                          name="b_logf")
                gs, = _mm(h_in, w_g, 0, 0, d, epi="sigmoid", out_dtypes=(BF16,), name="b_gate")
                proj[nm] = (q, kf, kb, vf, vb, lf[:, :nh], gs)
            q, kf, kb, vf, vb, lf, mul_p = proj["p"]
            lf3 = lf.reshape(bsz, seq, nh)
            op = _fox_attn_prompt(q, kb, vb, _cumsum_rows(lf3) * LOG2E, bsz, seq, hd)
            b_out["kp"].append(kf.reshape(bsz, seq, nh, hd))
            b_out["vp"].append(vf.reshape(bsz, seq, nh, hd))
            b_out["lp"].append(lf3)
            q, kf, kb, vf, vb, lf, mul_s = proj["s"]
            past = cache_b_k.shape[2]
            lf3 = lf.reshape(dbsz, dseq, nh)
            lf_all = jnp.concatenate([cache_b_logf[j].astype(F32), lf3], axis=1)
            cum = _cumsum_rows(lf_all).transpose(0, 2, 1) * LOG2E
            os_ = _fox_attn_sample(q, kb, vb, cache_b_k, cache_b_v, j,
                                   cum[:, :, past:, None], cum[:, :, None, :past],
                                   cum[:, :, None, past:], dseq, hd)
            b_out["ks"].append(kf.reshape(dbsz, dseq, nh, hd))
            b_out["vs"].append(vf.reshape(dbsz, dseq, nh, hd))
            b_out["ls"].append(lf3)
            w_out = b_w_out
        xp, = _mm(op, w_out, j, 0, d, epi="resid", mul=mul_p, xres=xp, gate=mod_p[2],
                  rows_per_seq=seq, out_dtypes=(F32,), name="out_proj")
        xs, = _mm(os_, w_out, j, 0, d, epi="resid", mul=mul_s, xres=xs, gate=mod_s[2],
                  rows_per_seq=ms, out_dtypes=(F32,), name="out_proj")

        cnt0 = jnp.zeros((ne, LANE), F32)
        h2p, h3p, eip, ewp, rkp, cnt1 = _router(xp, n2g, i, mod_p[3], mod_p[4], seq, router_wt, router_b3, cnt0)
        h2s, h3s, eis, ews, rks, cnt2 = _router(xs, n2g, i, mod_s[3], mod_s[4], ms, router_wt, router_b3, cnt1)
        h3 = jnp.concatenate([h3p, h3s], axis=0)
        ei = jnp.concatenate([eip, eis], axis=1)
        rank = jnp.concatenate([rkp, rks], axis=1)
        slot_token, pos, tile_expert, n_valid = _moe_plan(ei, rank, cnt2[:, 0].astype(jnp.int32), MOE_TILE)
        ys = _experts(h3, slot_token, tile_expert, n_valid, exp_w_gate, exp_w_up, exp_w_down, i)
        sp, = _mm(h2p, sh_gu, i, 0, sh_gu.shape[-1], epi="swiglu", out_dtypes=(BF16,), name="shared_up")
        ss, = _mm(h2s, sh_gu, i, 0, sh_gu.shape[-1], epi="swiglu", out_dtypes=(BF16,), name="shared_up")
        xp = _combine(xp, ys, pos[:, :mp], ewp.T, sp, sh_w_down, i, mod_p[5], seq)
        xs = _combine(xs, ys, pos[:, mp:], ews.T, ss, sh_w_down, i, mod_s[5], ms)

    st = jnp.stack
    return (xp.reshape(bsz, seq, d), xs.reshape(dbsz, dseq, d),
            st(a_out["kp"]), st(a_out["vp"]), st(a_out["ks"]), st(a_out["vs"]),
            st(b_out["kp"]), st(b_out["vp"]), st(b_out["lp"]),
            st(b_out["ks"]), st(b_out["vs"]), st(b_out["ls"]))
```

```python
import functools

import jax
import jax.numpy as jnp
import numpy as np
from jax import lax
from jax.experimental import pallas as pl
from jax.experimental.pallas import tpu as pltpu

F32 = jnp.float32
BF16 = jnp.bfloat16

CHUNK = 64
N_PREV_CHUNKS = 8
TOP_K = 8
N_GROUPS = 8
TOPK_GROUPS = 4
ROUTED_SCALE = 2.5
NORM_EPS = 1e-6
NEG_INF = -1e30
REMOVED = -3e38
LANE = 128
VMEM_LIMIT = 56 * 1024 * 1024
Q_TILE = 128
BAND_KEYS = (N_PREV_CHUNKS + 2) * CHUNK
FOX_TILE = 256
MOE_TILE = 512
LOG2E = 1.4426950408889634
CHUNK_HEADS = 4
FOX_HEADS = 4


def _pick(n, cands):
    for c in cands:
        if n % c == 0:
            return c
    return n


def _params(sem):
    return pltpu.CompilerParams(dimension_semantics=sem, vmem_limit_bytes=VMEM_LIMIT)


def _nt(a, b):
    return lax.dot_general(a, b, (((1,), (1,)), ((), ())), preferred_element_type=F32)


def _dot(a, b):
    return jnp.dot(a, b, preferred_element_type=F32)


def _mod_spec(mod, tm, tn, rows_per_seq, row_arg, col_arg):
    if mod.ndim == 3:
        def imap(*g):
            col = 0 if col_arg is None else g[col_arg]
            return ((g[row_arg] * tm) // rows_per_seq, 0, col)
        return pl.BlockSpec((None, 1, tn), imap)

    def imap2(*g):
        col = 0 if col_arg is None else g[col_arg]
        return (g[row_arg], col)
    return pl.BlockSpec((tm, tn), imap2)


def _adaln_kernel(c_ref, w_ref, b_ref, o_ref):
    c = c_ref[...]
    a = (c * jax.nn.sigmoid(c)).astype(BF16)
    o_ref[...] = _dot(a, w_ref[...].astype(BF16)) + b_ref[...]


def _adaln(c_all, ada_w, ada_b3, layer):
    r, d = c_all.shape
    n = ada_w.shape[2]
    tn = _pick(n, (1024, 512, 256, 128))
    return pl.pallas_call(
        _adaln_kernel,
        grid=(n // tn,),
        in_specs=[pl.BlockSpec((r, d), lambda j: (0, 0)),
                  pl.BlockSpec((None, d, tn), lambda j: (layer, 0, j)),
                  pl.BlockSpec((None, 1, tn), lambda j: (layer, 0, j))],
        out_specs=pl.BlockSpec((r, tn), lambda j: (0, j)),
        out_shape=jax.ShapeDtypeStruct((r, n), F32),
        compiler_params=_params(("arbitrary",)),
        name="adaln",
    )(c_all, ada_w, ada_b3)


def _modulated(x, g, sh, sc):
    ms = jnp.mean(x * x, axis=-1, keepdims=True)
    y = x * lax.rsqrt(ms + NORM_EPS) * g
    return y * (1.0 + sc) + sh


def _modulate_kernel(x_ref, g_ref, sh_ref, sc_ref, o_ref):
    o_ref[...] = _modulated(x_ref[...], g_ref[...], sh_ref[...], sc_ref[...]).astype(BF16)


def _modulate(x, g3, layer, shift, scale, rows_per_seq):
    m, d = x.shape
    tm = _pick(min(m, rows_per_seq), (512, 256, 128, 64, 32, 16, 8))
    return pl.pallas_call(
        _modulate_kernel,
        grid=(m // tm,),
        in_specs=[pl.BlockSpec((tm, d), lambda i: (i, 0)),
                  pl.BlockSpec((None, 1, d), lambda i: (layer, 0, 0)),
                  _mod_spec(shift, tm, d, rows_per_seq, 0, None),
                  _mod_spec(scale, tm, d, rows_per_seq, 0, None)],
        out_specs=pl.BlockSpec((tm, d), lambda i: (i, 0)),
        out_shape=jax.ShapeDtypeStruct((m, d), BF16),
        compiler_params=_params(("arbitrary",)),
        name="modulate",
    )(x, g3, shift, scale)


def _mm_kernel(*refs, epi, has_mul, n_out, scale):
    it = iter(refs)
    a_ref = next(it)
    m_ref = next(it) if has_mul else None
    w_ref = next(it)
    extra = []
    n_extra = {"headnorm": 1, "plain": 0, "logsig": 1, "sigmoid": 0, "resid": 2, "swiglu": 0}[epi]
    for _ in range(n_extra):
        extra.append(next(it))
    outs = [next(it) for _ in range(n_out)]
    wb_ref = next(it)

    @pl.when(pl.program_id(1) == 0)
    def _():
        wb_ref[...] = w_ref[...].astype(BF16)

    a = a_ref[...]
    if has_mul:
        a = a * m_ref[...]
    acc = _dot(a, wb_ref[...])

    if epi == "headnorm":
        g = extra[0][...]
        hd = g.shape[-1]
        ys = []
        for h in range(acc.shape[1] // hd):
            blk = acc[:, h * hd:(h + 1) * hd]
            ms = jnp.mean(blk * blk, axis=-1, keepdims=True)
            ys.append(blk * lax.rsqrt(ms + NORM_EPS) * g)
        res = jnp.concatenate(ys, axis=1)
    elif epi == "plain":
        res = acc
    elif epi == "logsig":
        z = acc + extra[0][...]
        res = jnp.minimum(z, 0.0) - jnp.log1p(jnp.exp(-jnp.abs(z)))
    elif epi == "sigmoid":
        res = jax.nn.sigmoid(acc)
    elif epi == "resid":
        res = extra[0][...] + extra[1][...] * acc
    elif epi == "swiglu":
        f = acc.shape[1] // 2
        ga = acc[:, :f]
        res = ga * jax.nn.sigmoid(ga) * acc[:, f:]
    for o in outs:
        val = res * scale if (o.dtype == BF16 and scale != 1.0) else res
        if len(o.shape) == 3:
            val = val.reshape(o.shape)
        o[...] = val.astype(o.dtype)


def _mm(a, w, layer, col0, n, *, epi, out_dtypes, mul=None, g=None, bias=None, xres=None,
        gate=None, rows_per_seq=None, scale=1.0, head_dim=None, name="mm"):
    m, k = a.shape
    if epi == "swiglu":
        tn = n
    else:
        tn = _pick(n, (1024, 512, 256, 128))
    assert col0 % tn == 0
    cb = col0 // tn
    tm = _pick(m, (512, 256, 128))
    if rows_per_seq is not None:
        assert rows_per_seq % tm == 0 or (gate is not None and gate.ndim == 2)
    n_out_cols = n // 2 if epi == "swiglu" else n
    tno = tn // 2 if epi == "swiglu" else tn

    args = [a]
    specs = [pl.BlockSpec((tm, k), lambda j, i: (i, 0))]
    if mul is not None:
        args.append(mul)
        specs.append(pl.BlockSpec((tm, k), lambda j, i: (i, 0)))
    args.append(w)
    specs.append(pl.BlockSpec((None, k, tn), lambda j, i: (layer, 0, cb + j)))
    if epi == "headnorm":
        args.append(g)
        specs.append(pl.BlockSpec((None, 1, g.shape[-1]), lambda j, i: (layer, 0, 0)))
    elif epi == "logsig":
        args.append(bias)
        specs.append(pl.BlockSpec((None, 1, tn), lambda j, i: (layer, 0, j)))
    elif epi == "resid":
        args.append(xres)
        specs.append(pl.BlockSpec((tm, tn), lambda j, i: (i, j)))
        args.append(gate)
        specs.append(_mod_spec(gate, tm, tn, rows_per_seq, 1, 0))
    out_shape, out_specs = [], []
    for dt in out_dtypes:
        if head_dim is not None and dt == F32:
            out_shape.append(jax.ShapeDtypeStruct((m, n_out_cols // head_dim, head_dim), dt))
            out_specs.append(pl.BlockSpec((tm, tno // head_dim, head_dim), lambda j, i: (i, j, 0)))
        else:
            out_shape.append(jax.ShapeDtypeStruct((m, n_out_cols), dt))
            out_specs.append(pl.BlockSpec((tm, tno), lambda j, i: (i, j)))
    res = pl.pallas_call(
        functools.partial(_mm_kernel, epi=epi, has_mul=mul is not None, n_out=len(out_dtypes),
                          scale=scale),
        grid=(n // tn, m // tm),
        in_specs=specs,
        out_specs=out_specs,
        out_shape=out_shape,
        scratch_shapes=[pltpu.VMEM((k, tn), BF16)],
        compiler_params=_params(("arbitrary", "arbitrary")),
        name=name,
    )(*args)
    return res


def _chunk_prompt_kernel(q_ref, k_ref, v_ref, b_ref, o_ref, *, n_tiles, edge, hd):
    heads = q_ref.shape[1] // hd

    def body(m, carry):
        q0 = pl.multiple_of(m * Q_TILE, Q_TILE)
        k0 = pl.multiple_of(jnp.maximum(m - edge, 0) * Q_TILE, Q_TILE)
        bi = jnp.minimum(m, edge)
        sls = [slice(g * hd, (g + 1) * hd) for g in range(heads)]
        ss = [_nt(q_ref[pl.ds(q0, Q_TILE), sl], k_ref[pl.ds(k0, BAND_KEYS), sl]) for sl in sls]
        ps, ls = [], []
        for g in range(heads):
            s = ss[g] + b_ref[g, bi]
            mx = jnp.max(s, axis=-1, keepdims=True)
            p = jnp.exp2(s - mx)
            ls.append(jnp.sum(p, axis=-1, keepdims=True))
            ps.append(p.astype(BF16))
        for g in range(heads):
            o = _dot(ps[g], v_ref[pl.ds(k0, BAND_KEYS), sls[g]]) / ls[g]
            o_ref[pl.ds(q0, Q_TILE), sls[g]] = o.astype(BF16)
        return carry
    lax.fori_loop(0, n_tiles, body, 0)


def _chunk_bias_prompt(table):
    nh = table.shape[0]
    rel_max = (table.shape[1] - 1) // 2
    edge = (BAND_KEYS - Q_TILE) // Q_TILE
    mm = np.arange(edge + 1)[:, None, None]
    qpos = mm * Q_TILE + np.arange(Q_TILE)[None, :, None]
    kpos = np.maximum(mm - edge, 0) * Q_TILE + np.arange(BAND_KEYS)[None, None, :]
    qc, kc = qpos // CHUNK, kpos // CHUNK
    vis = (kc <= qc) & (kc >= qc - N_PREV_CHUNKS)
    period = Q_TILE + BAND_KEYS - 1
    c_m = np.minimum(np.arange(edge + 1), edge) * Q_TILE
    n = np.arange(period)
    n = np.where(n < BAND_KEYS, n, n - period)[None, :]
    idx = np.clip(c_m[:, None] - n, -rel_max, rel_max) + rel_max
    w = table[:, idx].astype(F32) * LOG2E
    skew = jnp.tile(w, (1, 1, Q_TILE))[..., :Q_TILE * (period - 1)]
    b = skew.reshape(nh, edge + 1, Q_TILE, period - 1)[..., :BAND_KEYS]
    return jnp.where(vis[None], b, NEG_INF), edge


def _chunk_attn_prompt(q, k, v, bias, edge, bsz, seq, hd):
    d = q.shape[1]
    nh = d // hd
    heads = _pick(nh, (CHUNK_HEADS, 2, 1))
    q3, k3, v3 = (t.reshape(bsz, seq, d) for t in (q, k, v))
    spec = pl.BlockSpec((None, seq, heads * hd), lambda b, h: (b, 0, h))
    o = pl.pallas_call(
        functools.partial(_chunk_prompt_kernel, n_tiles=seq // Q_TILE, edge=edge, hd=hd),
        grid=(bsz, nh // heads),
        in_specs=[spec, spec, spec,
                  pl.BlockSpec((heads, edge + 1, Q_TILE, BAND_KEYS), lambda b, h: (h, 0, 0, 0))],
        out_specs=spec,
        out_shape=jax.ShapeDtypeStruct((bsz, seq, d), BF16),
        compiler_params=_params(("arbitrary", "arbitrary")),
        name="chunk_attn_prompt",
    )(q3, k3, v3, bias)
    return o.reshape(bsz * seq, d)


def _sample_attn_kernel(*refs, hd, fox):
    if fox:
        q_ref, kn_ref, vn_ref, kc_ref, vc_ref, cq_ref, ckc_ref, ckn_ref, o_ref, ks_scr, vs_scr = refs
    else:
        q_ref, kn_ref, vn_ref, kc_ref, vc_ref, bc_ref, bn_ref, o_ref, ks_scr, vs_scr = refs
    t = q_ref.shape[0]
    nh = q_ref.shape[1] // hd
    past = kc_ref.shape[0]
    rows = _pick(past, (512, 256, 128))
    for c in range(past // rows):
        rs = slice(c * rows, (c + 1) * rows)
        ks_scr[rs, :] = kc_ref[rs].reshape(rows, nh * hd).astype(BF16)
        vs_scr[rs, :] = vc_ref[rs].reshape(rows, nh * hd).astype(BF16)
    if fox:
        row = lax.broadcasted_iota(jnp.int32, (t, t), 0)
        col = lax.broadcasted_iota(jnp.int32, (t, t), 1)
        causal = col <= row
    for h in range(nh):
        sl = slice(h * hd, (h + 1) * hd)
        q = q_ref[:, sl]
        kc = ks_scr[:, sl]
        vc = vs_scr[:, sl]
        s1 = _nt(q, kc)
        s2 = _nt(q, kn_ref[:, sl])
        if fox:
            cq = cq_ref[h]
            s1 = s1 + cq - ckc_ref[h]
            s2 = jnp.where(causal, s2 + cq - ckn_ref[h], NEG_INF)
        else:
            s1 = s1 + bc_ref[h]
            s2 = s2 + bn_ref[h]
        mx = jnp.maximum(jnp.max(s1, axis=-1, keepdims=True), jnp.max(s2, axis=-1, keepdims=True))
        p1 = jnp.exp2(s1 - mx)
        p2 = jnp.exp2(s2 - mx)
        l = jnp.sum(p1, axis=-1, keepdims=True) + jnp.sum(p2, axis=-1, keepdims=True)
        o = (_dot(p1.astype(BF16), vc) + _dot(p2.astype(BF16), vn_ref[:, sl])) / l
        o_ref[:, sl] = o.astype(BF16)


def _chunk_attn_sample(q, kn, vn, cache_k, cache_v, layer, bias_c, bias_n, t, hd):
    _, bsz, w, nh, _ = cache_k.shape
    d = nh * hd
    hg = _pick(nh, (8,)) * hd
    ng = d // hg
    tok = pl.BlockSpec((t, hg), lambda b, g: (b, g))
    cache = pl.BlockSpec((None, None, w, hg // hd, hd), lambda b, g: (layer, b, 0, g, 0))
    return pl.pallas_call(
        functools.partial(_sample_attn_kernel, hd=hd, fox=False),
        grid=(bsz, ng),
        in_specs=[tok, tok, tok, cache, cache,
                  pl.BlockSpec((hg // hd, t, w), lambda b, g: (g, 0, 0)),
                  pl.BlockSpec((hg // hd, t, t), lambda b, g: (g, 0, 0))],
        out_specs=tok,
        out_shape=jax.ShapeDtypeStruct((bsz * t, d), BF16),
        scratch_shapes=[pltpu.VMEM((w, hg), BF16), pltpu.VMEM((w, hg), BF16)],
        compiler_params=_params(("arbitrary", "arbitrary")),
        name="chunk_attn_sample",
    )(q, kn, vn, cache_k, cache_v, bias_c, bias_n)


def _fox_attn_sample(q, kn, vn, cache_k, cache_v, layer, cq, ckc, ckn, t, hd):
    _, bsz, w, nh, _ = cache_k.shape
    d = nh * hd
    hg = _pick(nh, (8,)) * hd
    ng = d // hg
    gh = hg // hd
    tok = pl.BlockSpec((t, hg), lambda b, g: (b, g))
    cache = pl.BlockSpec((None, None, w, gh, hd), lambda b, g: (layer, b, 0, g, 0))
    return pl.pallas_call(
        functools.partial(_sample_attn_kernel, hd=hd, fox=True),
        grid=(bsz, ng),
        in_specs=[tok, tok, tok, cache, cache,
                  pl.BlockSpec((None, gh, t, 1), lambda b, g: (b, g, 0, 0)),
                  pl.BlockSpec((None, gh, 1, w), lambda b, g: (b, g, 0, 0)),
                  pl.BlockSpec((None, gh, 1, t), lambda b, g: (b, g, 0, 0))],
        out_specs=tok,
        out_shape=jax.ShapeDtypeStruct((bsz * t, d), BF16),
        scratch_shapes=[pltpu.VMEM((w, hg), BF16), pltpu.VMEM((w, hg), BF16)],
        compiler_params=_params(("arbitrary", "arbitrary")),
        name="fox_attn_sample",
    )(q, kn, vn, cache_k, cache_v, cq, ckc, ckn)


def _fox_prompt_kernel(q_ref, k_ref, v_ref, cq_ref, ck_ref, o_ref, *scr):
    i = pl.program_id(2)
    tq = q_ref.shape[0]
    heads = cq_ref.shape[0]
    hd = q_ref.shape[1] // heads
    m_scr, l_scr, acc_scr = scr[:heads], scr[heads:2 * heads], scr[2 * heads:3 * heads]
    qt_scr = scr[3 * heads:]
    row = lax.broadcasted_iota(jnp.int32, (tq, tq), 0)
    col = lax.broadcasted_iota(jnp.int32, (tq, tq), 1)
    for g in range(heads):
        m_scr[g][...] = jnp.full((1, tq), NEG_INF, F32)
        l_scr[g][...] = jnp.zeros((1, tq), F32)
        acc_scr[g][...] = jnp.zeros(acc_scr[g].shape, F32)
        qt_scr[g][...] = q_ref[:, g * hd:(g + 1) * hd].astype(F32).T.astype(BF16)

    def step(j, diag):
        k0 = pl.multiple_of(j * tq, tq)
        sts = [_dot(k_ref[pl.ds(k0, tq), g * hd:(g + 1) * hd], qt_scr[g][...]) for g in range(heads)]
        pts, alphas = [], []
        for g in range(heads):
            ck = ck_ref[g, pl.ds(k0, tq), :]
            st = sts[g] + cq_ref[g] - jnp.concatenate([ck] * (tq // LANE), axis=1)
            if diag:
                st = jnp.where(row <= col, st, NEG_INF)
            m_old = m_scr[g][...]
            m_new = jnp.maximum(m_old, jnp.max(st, axis=0, keepdims=True))
            alpha = jnp.exp2(m_old - m_new)
            pt = jnp.exp2(st - m_new)
            l_scr[g][...] = alpha * l_scr[g][...] + jnp.sum(pt, axis=0, keepdims=True)
            m_scr[g][...] = m_new
            pts.append(pt.astype(BF16))
            alphas.append(alpha)
        for g in range(heads):
            vb = v_ref[pl.ds(k0, tq), g * hd:(g + 1) * hd]
            pv = lax.dot_general(vb, pts[g], (((0,), (0,)), ((), ())), preferred_element_type=F32)
            acc_scr[g][...] = alphas[g] * acc_scr[g][...] + pv

    def body(j, carry):
        step(j, False)
        return carry
    lax.fori_loop(0, i, body, 0)
    step(i, True)
    for g in range(heads):
        o_ref[:, g * hd:(g + 1) * hd] = (acc_scr[g][...] / l_scr[g][...]).T.astype(BF16)


def _cumsum_rows(x):
    b, s, h = x.shape
    pad = -s % LANE
    xb = jnp.pad(x, ((0, 0), (0, pad), (0, 0))).reshape(b, (s + pad) // LANE, LANE, h)
    tri = jnp.tril(jnp.ones((LANE, LANE), F32))
    within = jnp.einsum("ij,bnjh->bnih", tri, xb, precision=lax.Precision.HIGHEST)
    tot = within[:, :, -1, :]
    offs = jnp.cumsum(tot, axis=1) - tot
    return (within + offs[:, :, None, :]).reshape(b, s + pad, h)[:, :s]


def _fox_attn_prompt(q, k, v, cum, bsz, seq, hd):
    d = q.shape[1]
    nh = d // hd
    heads = _pick(nh, (FOX_HEADS, 2, 1))
    tq = _pick(seq, (FOX_TILE, 128))
    cum_t = cum.transpose(0, 2, 1)
    cq = cum_t[:, :, None, :]
    ck = jnp.broadcast_to(cum_t[..., None], (bsz, nh, seq, LANE))
    q3, k3, v3 = (t.reshape(bsz, seq, d) for t in (q, k, v))
    qspec = pl.BlockSpec((None, tq, heads * hd), lambda b, h, i: (b, i, h))
    kspec = pl.BlockSpec((None, seq, heads * hd), lambda b, h, i: (b, 0, h))
    o = pl.pallas_call(
        _fox_prompt_kernel,
        grid=(bsz, nh // heads, seq // tq),
        in_specs=[qspec, kspec, kspec,
                  pl.BlockSpec((None, heads, 1, tq), lambda b, h, i: (b, h, 0, i)),
                  pl.BlockSpec((None, heads, seq, LANE), lambda b, h, i: (b, h, 0, 0))],
        out_specs=qspec,
        out_shape=jax.ShapeDtypeStruct((bsz, seq, d), BF16),
        scratch_shapes=([pltpu.VMEM((1, tq), F32)] * (2 * heads)
                        + [pltpu.VMEM((hd, tq), F32)] * heads
                        + [pltpu.VMEM((hd, tq), BF16)] * heads),
        compiler_params=_params(("arbitrary", "arbitrary", "arbitrary")),
        name="fox_attn_prompt",
    )(q3, k3, v3, cq, ck)
    return o.reshape(bsz * seq, d)


def _router_kernel(x_ref, g_ref, sh_ref, sc_ref, wt_ref, rb_ref, cin_ref,
                   h_ref, h3_ref, ei_ref, ew_ref, rk_ref, cout_ref, cnt):
    @pl.when(pl.program_id(0) == 0)
    def _():
        cnt[...] = cin_ref[...]

    hf = _modulated(x_ref[...], g_ref[...], sh_ref[...], sc_ref[...])
    hh = hf.astype(BF16)
    h_ref[...] = hh
    h3_ref[...] = hf.reshape(h3_ref.shape).astype(BF16)
    hl = (hf - hh.astype(F32)).astype(BF16)
    w = wt_ref[...]
    wh = w.astype(BF16)
    wl = (w - wh.astype(F32)).astype(BF16)
    logits = _nt(wh, hh) + (_nt(wh, hl) + _nt(wl, hh))
    scores = jax.nn.sigmoid(logits)
    sel = scores + rb_ref[...]
    ne, tm = sel.shape
    gsz = ne // N_GROUPS
    row_g = lax.broadcasted_iota(jnp.int32, (gsz, tm), 0).astype(F32)
    row_n = lax.broadcasted_iota(jnp.int32, (N_GROUPS, tm), 0).astype(F32)
    row_e = lax.broadcasted_iota(jnp.int32, (ne, tm), 0).astype(F32)

    def first_max(x, rows, big):
        mx = jnp.max(x, axis=0, keepdims=True)
        ix = jnp.min(jnp.where(x == mx, rows, big), axis=0, keepdims=True)
        return mx, ix

    grp = jnp.zeros((N_GROUPS, tm), F32)
    for g in range(N_GROUPS):
        blk = sel[g * gsz:(g + 1) * gsz, :]
        m1, i1 = first_max(blk, row_g, float(gsz))
        m2 = jnp.max(jnp.where(row_g == i1, REMOVED, blk), axis=0, keepdims=True)
        grp = jnp.where(row_n == float(g), m1 + m2, grp)
    gsel = jnp.zeros((N_GROUPS, tm), F32)
    work = grp
    for _ in range(TOPK_GROUPS):
        _, ix = first_max(work, row_n, float(N_GROUPS))
        hit = row_n == ix
        gsel = jnp.where(hit, 1.0, gsel)
        work = jnp.where(hit, REMOVED, work)
    work = jnp.concatenate(
        [jnp.where(gsel[g:g + 1, :] > 0.0, sel[g * gsz:(g + 1) * gsz, :], NEG_INF)
         for g in range(N_GROUPS)], axis=0)
    row_k = lax.broadcasted_iota(jnp.int32, (TOP_K, tm), 0)
    ei = jnp.zeros((TOP_K, tm), F32)
    ew = jnp.zeros((TOP_K, tm), F32)
    hits = []
    chosen = jnp.zeros((ne, tm), F32)
    for r in range(TOP_K):
        _, ix = first_max(work, row_e, float(ne))
        hit = row_e == ix
        hits.append(hit)
        chosen = jnp.where(hit, 1.0, chosen)
        wr = jnp.sum(jnp.where(hit, scores, 0.0), axis=0, keepdims=True)
        ei = jnp.where(row_k == r, ix, ei)
        ew = jnp.where(row_k == r, wr, ew)
        work = jnp.where(hit, REMOVED, work)
    ew = ew / jnp.sum(ew, axis=0, keepdims=True) * ROUTED_SCALE
    ei_ref[...] = ei.astype(jnp.int32)
    ew_ref[...] = ew
    cb = chosen.astype(BF16)
    tr = lax.broadcasted_iota(jnp.int32, (tm, tm), 0)
    tc = lax.broadcasted_iota(jnp.int32, (tm, tm), 1)
    before = _dot(cb, jnp.where(tr < tc, 1.0, 0.0).astype(BF16))
    total = _dot(cb, jnp.ones((tm, LANE), BF16))
    base = cnt[...]
    rank_e = jnp.concatenate([base] * (tm // LANE), axis=1) + before
    rk = jnp.zeros((TOP_K, tm), F32)
    for r in range(TOP_K):
        rr = jnp.sum(jnp.where(hits[r], rank_e, 0.0), axis=0, keepdims=True)
        rk = jnp.where(row_k == r, rr, rk)
    rk_ref[...] = rk.astype(jnp.int32)
    cnt[...] = base + total
    cout_ref[...] = base + total


def _router(x, g3, layer, shift, scale, rows_per_seq, router_wt, router_b3, cnt_in):
    m, d = x.shape
    ne = router_wt.shape[1]
    tm = _pick(m, (256, 128))
    asg = pl.BlockSpec((TOP_K, tm), lambda i: (0, i))
    return pl.pallas_call(
        _router_kernel,
        grid=(m // tm,),
        in_specs=[pl.BlockSpec((tm, d), lambda i: (i, 0)),
                  pl.BlockSpec((None, 1, d), lambda i: (layer, 0, 0)),
                  _mod_spec(shift, tm, d, rows_per_seq, 0, None),
                  _mod_spec(scale, tm, d, rows_per_seq, 0, None),
                  pl.BlockSpec((None, ne, d), lambda i: (layer, 0, 0)),
                  pl.BlockSpec((None, ne, 1), lambda i: (layer, 0, 0)),
                  pl.BlockSpec((ne, LANE), lambda i: (0, 0))],
        out_specs=[pl.BlockSpec((tm, d), lambda i: (i, 0)),
                   pl.BlockSpec((tm, d // LANE, LANE), lambda i: (i, 0, 0)), asg, asg, asg,
                   pl.BlockSpec((ne, LANE), lambda i: (0, 0))],
        out_shape=[jax.ShapeDtypeStruct((m, d), BF16),
                   jax.ShapeDtypeStruct((m, d // LANE, LANE), BF16),
                   jax.ShapeDtypeStruct((TOP_K, m), jnp.int32),
                   jax.ShapeDtypeStruct((TOP_K, m), F32),
                   jax.ShapeDtypeStruct((TOP_K, m), jnp.int32),
                   jax.ShapeDtypeStruct((ne, LANE), F32)],
        scratch_shapes=[pltpu.VMEM((ne, LANE), F32)],
        compiler_params=_params(("arbitrary",)),
        name="router",
    )(x, g3, shift, scale, router_wt, router_b3, cnt_in)


def _experts_kernel(te_ref, nv_ref, tok_ref, nxt_ref, h_ref, wg_ref, wu_ref, wd_ref, y_ref,
                    wgb, wub, wdb, xbuf, sem):
    t = pl.program_id(0)
    nv = nv_ref[0]
    tm = xbuf.shape[1]
    d = wg_ref.shape[0]
    slot = lax.rem(t, 2)
    e = te_ref[t]
    prev = te_ref[jnp.maximum(t - 1, 0)]

    def row_copy(idx_ref, r, s):
        return pltpu.make_async_copy(h_ref.at[idx_ref[0, r]], xbuf.at[s, r], sem.at[s])

    @pl.when(t == 0)
    def _():
        for r in range(tm):
            row_copy(tok_ref, r, 0).start(priority=r % 2)

    @pl.when((t == 0) | (e != prev))
    def _():
        wgb[...] = wg_ref[...].astype(BF16)
        wub[...] = wu_ref[...].astype(BF16)
        wdb[...] = wd_ref[...].astype(BF16)

    @pl.when(t < nv)
    def _():
        for r in range(tm):
            row_copy(tok_ref, r, slot).wait()
        for r in range(tm):
            row_copy(nxt_ref, r, 1 - slot).start(priority=r % 2)
        x = xbuf[slot].astype(F32).reshape(tm, d).astype(BF16)
        a = _dot(x, wgb[...])
        b = _dot(x, wub[...])
        s = (a * jax.nn.sigmoid(a) * b).astype(BF16)
        y_ref[...] = _dot(s, wdb[...]).reshape(y_ref.shape).astype(BF16)

    @pl.when(t == nv - 1)
    def _():
        for r in range(tm):
            row_copy(nxt_ref, r, 1 - slot).wait()

    @pl.when(t >= nv)
    def _():
        y_ref[...] = jnp.zeros(y_ref.shape, BF16)


def _experts(h3, slot_token, tile_expert, n_valid, wg, wu, wd, layer):
    _, nl, lane = h3.shape
    d = nl * lane
    f = wg.shape[-1]
    tm = MOE_TILE
    n_tiles = slot_token.shape[0] // tm
    tok3 = slot_token.reshape(n_tiles, 1, tm)

    def tok_spec(shift):
        return pl.BlockSpec((None, 1, tm), lambda t, te, nv: (jnp.minimum(t + shift, nv[0] - 1), 0, 0),
                            memory_space=pltpu.SMEM)
    grid_spec = pltpu.PrefetchScalarGridSpec(
        num_scalar_prefetch=2,
        grid=(n_tiles,),
        in_specs=[tok_spec(0), tok_spec(1),
                  pl.BlockSpec(memory_space=pl.ANY),
                  pl.BlockSpec((None, None, d, f), lambda t, te, nv: (layer, te[t], 0, 0)),
                  pl.BlockSpec((None, None, d, f), lambda t, te, nv: (layer, te[t], 0, 0)),
                  pl.BlockSpec((None, None, f, d), lambda t, te, nv: (layer, te[t], 0, 0))],
        out_specs=pl.BlockSpec((tm, nl, lane), lambda t, te, nv: (t, 0, 0)),
        scratch_shapes=[pltpu.VMEM((d, f), BF16), pltpu.VMEM((d, f), BF16), pltpu.VMEM((f, d), BF16),
                        pltpu.VMEM((2, tm, nl, lane), BF16), pltpu.SemaphoreType.DMA((2,))],
    )
    return pl.pallas_call(
        _experts_kernel,
        grid_spec=grid_spec,
        out_shape=jax.ShapeDtypeStruct((n_tiles * tm, nl, lane), BF16),
        compiler_params=_params(("arbitrary",)),
        name="experts",
    )(tile_expert, n_valid, tok3, tok3, h3, wg, wu, wd)


def _moe_plan(ei, rank, counts, tm):
    kk, t = ei.shape
    ne = counts.shape[0]
    n_asg = kk * t
    n_tiles = n_asg // tm + ne
    i32 = jnp.int32
    ar = jnp.arange(ne, dtype=i32)
    tiles_per = (counts + tm - 1) // tm
    tile_end = jnp.cumsum(tiles_per)
    pstart = (tile_end - tiles_per) * tm
    ustart = jnp.cumsum(counts) - counts
    pos = rank + jnp.sum(jnp.where(ei[..., None] == ar, pstart, 0), axis=-1)
    n_valid = tile_end[-1]
    tile_ids = jnp.arange(n_tiles, dtype=i32)
    te = jnp.sum((tile_end[None, :] <= tile_ids[:, None]).astype(i32), axis=1)
    te = jnp.minimum(te, jnp.sum((tile_end <= n_valid - 1).astype(i32)))
    onehot = te[:, None] == ar[None, :]
    t_pstart, t_count, t_ustart = (jnp.sum(jnp.where(onehot, v, 0), axis=1)
                                   for v in (pstart, counts, ustart))
    within = tile_ids[:, None] * tm + jnp.arange(tm, dtype=i32)[None, :] - t_pstart[:, None]
    valid = (within < t_count[:, None]) & (tile_ids[:, None] < n_valid)
    order = jnp.argsort(ei.T.reshape(-1), stable=True)
    src = jnp.take(order, jnp.clip(t_ustart[:, None] + within, 0, n_asg - 1).reshape(-1)) // kk
    slot_token = jnp.where(valid.reshape(-1), src, 0).astype(i32)
    return slot_token, pos.astype(i32), te.astype(i32), n_valid.reshape(1).astype(i32)


def _combine_kernel(pos_ref, x_ref, ys_ref, w_ref, sa_ref, sd_ref, gate_ref, o_ref, sdb, ybuf, sem):
    @pl.when(pl.program_id(0) == 0)
    def _():
        sdb[...] = sd_ref[...].astype(BF16)

    tm, d = x_ref.shape

    def row_copy(k, r):
        return pltpu.make_async_copy(ys_ref.at[pos_ref[k, r]], ybuf.at[k * tm + r], sem)

    def issue(r, carry):
        for k in range(TOP_K):
            row_copy(k, r).start(priority=k % 2)
        return carry
    lax.fori_loop(0, tm, issue, 0)

    w = w_ref[...]
    acc = _dot(sa_ref[...], sdb[...])

    for r in range(tm):
        for k in range(TOP_K):
            row_copy(k, r).wait()

    for k in range(TOP_K):
        yk = ybuf[pl.ds(k * tm, tm)].astype(F32).reshape(tm, d)
        acc = acc + w[:, k:k + 1] * yk
    o_ref[...] = x_ref[...] + gate_ref[...] * acc


def _combine(x, ys, pos, w, sact, sh_w_down, layer, gate, rows_per_seq):
    m, d = x.shape
    f = sact.shape[1]
    tm = _pick(m, (256, 128))
    return pl.pallas_call(
        _combine_kernel,
        grid=(m // tm,),
        in_specs=[pl.BlockSpec((TOP_K, tm), lambda i: (0, i), memory_space=pltpu.SMEM),
                  pl.BlockSpec((tm, d), lambda i: (i, 0)),
                  pl.BlockSpec(memory_space=pl.ANY),
                  pl.BlockSpec((tm, TOP_K), lambda i: (i, 0)),
                  pl.BlockSpec((tm, f), lambda i: (i, 0)),
                  pl.BlockSpec((None, f, d), lambda i: (layer, 0, 0)),
                  _mod_spec(gate, tm, d, rows_per_seq, 0, None)],
        out_specs=pl.BlockSpec((tm, d), lambda i: (i, 0)),
        out_shape=jax.ShapeDtypeStruct((m, d), F32),
        scratch_shapes=[pltpu.VMEM((f, d), BF16),
                        pltpu.VMEM((TOP_K * tm, d // LANE, LANE), BF16),
                        pltpu.SemaphoreType.DMA],
        compiler_params=_params(("arbitrary",)),
        name="combine",
    )(pos, x, ys, w, sact, sh_w_down, gate)


def kernel(x_prompt, x_sample, c_prompt, c_sample, cache_a_k, cache_a_v, cache_b_k, cache_b_v,
           cache_b_logf, ada_w, ada_b, norm1_g, norm2_g, a_w_in, a_q_g, a_k_g, a_rel_table, a_w_out,
           b_w_in, b_f_bias, b_q_g, b_k_g, b_w_out, router_w, router_bias, exp_w_gate, exp_w_up,
           exp_w_down, sh_w_gate, sh_w_up, sh_w_down):
    bsz, seq, d = x_prompt.shape
    dbsz, dseq, _ = x_sample.shape
    depth = ada_w.shape[0]
    hd = a_q_g.shape[-1]
    nh = d // hd
    ne = router_w.shape[-1]
    mp, ms = bsz * seq, dbsz * dseq
    attn_scale = hd ** -0.5 * LOG2E

    xp = x_prompt.reshape(mp, d)
    xs = x_sample.reshape(ms, d)
    c_all = jnp.concatenate([c_prompt, c_sample], axis=0)
    ada_b3 = ada_b.reshape(depth, 1, -1)
    n1g = norm1_g.reshape(depth, 1, d)
    n2g = norm2_g.reshape(depth, 1, d)
    router_wt = jnp.swapaxes(router_w, 1, 2)
    router_b3 = router_bias.reshape(depth, ne, 1)
    sh_gu = jnp.concatenate([sh_w_gate, sh_w_up], axis=-1)

    a_out = {n: [] for n in ("kp", "vp", "ks", "vs")}
    b_out = {n: [] for n in ("kp", "vp", "lp", "ks", "vs", "ls")}

    for i in range(depth):
        mods = _adaln(c_all, ada_w, ada_b3, i)
        mod6 = jnp.split(mods, 6, axis=-1)
        mod_p = [t[:bsz].reshape(bsz, 1, d) for t in mod6]
        mod_s = [jnp.repeat(t[bsz:], dseq, axis=0) for t in mod6]
        hp = _modulate(xp, n1g, i, mod_p[0], mod_p[1], seq)
        hs = _modulate(xs, n1g, i, mod_s[0], mod_s[1], ms)
        j = i // 2
        if i % 2 == 0:
            qg, kg = a_q_g.reshape(-1, 1, hd), a_k_g.reshape(-1, 1, hd)
            proj = {}
            for nm, h_in in (("p", hp), ("s", hs)):
                q, = _mm(h_in, a_w_in, j, 0, d, epi="headnorm", g=qg, scale=attn_scale,
                         out_dtypes=(BF16,), name="a_q")
                kf, kb = _mm(h_in, a_w_in, j, d, d, epi="headnorm", g=kg,
                             out_dtypes=(F32, BF16), head_dim=hd, name="a_k")
                vf, vb = _mm(h_in, a_w_in, j, 2 * d, d, epi="plain",
                             out_dtypes=(F32, BF16), head_dim=hd, name="a_v")
                proj[nm] = (q, kf, kb, vf, vb)
            q, kf, kb, vf, vb = proj["p"]
            bias_p, edge = _chunk_bias_prompt(a_rel_table[j])
            op = _chunk_attn_prompt(q, kb, vb, bias_p, edge, bsz, seq, hd)
            keep = min(N_PREV_CHUNKS * CHUNK, seq)
            a_out["kp"].append(kf.reshape(bsz, seq, nh, hd)[:, seq - keep:])
            a_out["vp"].append(vf.reshape(bsz, seq, nh, hd)[:, seq - keep:])
            q, kf, kb, vf, vb = proj["s"]
            w = cache_a_k.shape[2]
            rel_max = (a_rel_table.shape[-1] - 1) // 2
            rel = np.arange(dseq)[:, None] + w - np.arange(w + dseq)[None, :]
            bias_s = a_rel_table[j][:, np.clip(rel, -rel_max, rel_max) + rel_max].astype(F32) * LOG2E
            os_ = _chunk_attn_sample(q, kb, vb, cache_a_k, cache_a_v, j,
                                     bias_s[:, :, :w], bias_s[:, :, w:], dseq, hd)
            a_out["ks"].append(kf.reshape(dbsz, dseq, nh, hd))
            a_out["vs"].append(vf.reshape(dbsz, dseq, nh, hd))
            mul_p = mul_s = None
            w_out = a_w_out
        else:
            qg, kg = b_q_g.reshape(-1, 1, hd), b_k_g.reshape(-1, 1, hd)
            w_f = jnp.pad(b_w_in[j][:, 3 * d:3 * d + nh], ((0, 0), (0, LANE - nh)))[None]
            f_b = jnp.pad(b_f_bias[j], (0, LANE - nh)).reshape(1, 1, LANE)
            w_g = b_w_in[j][:, 3 * d + nh:][None]
            proj = {}
            for nm, h_in in (("p", hp), ("s", hs)):
                q, = _mm(h_in, b_w_in, j, 0, d, epi="headnorm", g=qg, scale=attn_scale,
                         out_dtypes=(BF16,), name="b_q")
                kf, kb = _mm(h_in, b_w_in, j, d, d, epi="headnorm", g=kg,
                             out_dtypes=(F32, BF16), head_dim=hd, name="b_k")
                vf, vb = _mm(h_in, b_w_in, j, 2 * d, d, epi="plain",
                             out_dtypes=(F32, BF16), head_dim=hd, name="b_v")
                lf, = _mm(h_in, w_f, 0, 0, LANE, epi="logsig", bias=f_b, out_dtypes=(F32,),
                          name="b_logf")
                gs, = _mm(h_in, w_g, 0, 0, d, epi="sigmoid", out_dtypes=(BF16,), name="b_gate")
                proj[nm] = (q, kf, kb, vf, vb, lf[:, :nh], gs)
            q, kf, kb, vf, vb, lf, mul_p = proj["p"]
            lf3 = lf.reshape(bsz, seq, nh)
            op = _fox_attn_prompt(q, kb, vb, _cumsum_rows(lf3) * LOG2E, bsz, seq, hd)
            b_out["kp"].append(kf.reshape(bsz, seq, nh, hd))
            b_out["vp"].append(vf.reshape(bsz, seq, nh, hd))
            b_out["lp"].append(lf3)
            q, kf, kb, vf, vb, lf, mul_s = proj["s"]
            past = cache_b_k.shape[2]
            lf3 = lf.reshape(dbsz, dseq, nh)
            lf_all = jnp.concatenate([cache_b_logf[j].astype(F32), lf3], axis=1)
            cum = _cumsum_rows(lf_all).transpose(0, 2, 1) * LOG2E
            os_ = _fox_attn_sample(q, kb, vb, cache_b_k, cache_b_v, j,
                                   cum[:, :, past:, None], cum[:, :, None, :past],
                                   cum[:, :, None, past:], dseq, hd)
            b_out["ks"].append(kf.reshape(dbsz, dseq, nh, hd))
            b_out["vs"].append(vf.reshape(dbsz, dseq, nh, hd))
            b_out["ls"].append(lf3)
            w_out = b_w_out
        xp, = _mm(op, w_out, j, 0, d, epi="resid", mul=mul_p, xres=xp, gate=mod_p[2],
                  rows_per_seq=seq, out_dtypes=(F32,), name="out_proj")
        xs, = _mm(os_, w_out, j, 0, d, epi="resid", mul=mul_s, xres=xs, gate=mod_s[2],
                  rows_per_seq=ms, out_dtypes=(F32,), name="out_proj")

        cnt0 = jnp.zeros((ne, LANE), F32)
        h2p, h3p, eip, ewp, rkp, cnt1 = _router(xp, n2g, i, mod_p[3], mod_p[4], seq, router_wt, router_b3, cnt0)
        h2s, h3s, eis, ews, rks, cnt2 = _router(xs, n2g, i, mod_s[3], mod_s[4], ms, router_wt, router_b3, cnt1)
        h3 = jnp.concatenate([h3p, h3s], axis=0)
        ei = jnp.concatenate([eip, eis], axis=1)
        rank = jnp.concatenate([rkp, rks], axis=1)
        slot_token, pos, tile_expert, n_valid = _moe_plan(ei, rank, cnt2[:, 0].astype(jnp.int32), MOE_TILE)
        ys = _experts(h3, slot_token, tile_expert, n_valid, exp_w_gate, exp_w_up, exp_w_down, i)
        sp, = _mm(h2p, sh_gu, i, 0, sh_gu.shape[-1], epi="swiglu", out_dtypes=(BF16,), name="shared_up")
        ss, = _mm(h2s, sh_gu, i, 0, sh_gu.shape[-1], epi="swiglu", out_dtypes=(BF16,), name="shared_up")
        xp = _combine(xp, ys, pos[:, :mp], ewp.T, sp, sh_w_down, i, mod_p[5], seq)
        xs = _combine(xs, ys, pos[:, mp:], ews.T, ss, sh_w_down, i, mod_s[5], ms)

    st = jnp.stack
    return (xp.reshape(bsz, seq, d), xs.reshape(dbsz, dseq, d),
            st(a_out["kp"]), st(a_out["vp"]), st(a_out["ks"]), st(a_out["vs"]),
            st(b_out["kp"]), st(b_out["vp"]), st(b_out["lp"]),
            st(b_out["ks"]), st(b_out["vs"]), st(b_out["ls"]))
```
